```python
import jax, jax.numpy as jnp
from jax import lax
import numpy as np

D_MODEL = 1024
BATCH = 16
SEQ = 2048
DEPTH = 1
DEC_BATCH = 128
DEC_SEQ = 1
PAST_LEN = 8192
PAGE_SIZE = 128

N_HEADS = 8
HEAD_DIM = 64
ATTN_WIDTH = N_HEADS * HEAD_DIM
MOBA_BLOCK = 256
MOBA_TOPK = 3
Q_BLOCK = 16
CHUNK = 128
SGU_GROUPS = 4
SGU_WIDTH = 512
SGU_GROUP_WIDTH = SGU_WIDTH // SGU_GROUPS
D_FF = ((8 * D_MODEL // 3 + 255) // 256) * 256
ALIBI_MAX_BIAS = 8.0
EPS = 1e-6

kernel_name = 'hybrid_sgu_moba_decoder_step'


def rms_norm(x, g):
    xf = x.astype(jnp.float32)
    y = xf * lax.rsqrt(jnp.mean(xf * xf, axis=-1, keepdims=True) + EPS)
    return (y * g.astype(jnp.float32)).astype(x.dtype)


def alibi_slopes():
    return jnp.exp2(-(ALIBI_MAX_BIAS / N_HEADS) * jnp.arange(1, N_HEADS + 1, dtype=jnp.float32))


def spatial_gating(u, v_n, w_spatial, b_spatial):
    n, t = u.shape[0], u.shape[1]
    tp = -(-t // CHUNK) * CHUNK
    v = jnp.pad(v_n, ((0, 0), (0, tp - t), (0, 0)))
    v = v.reshape(n, tp // CHUNK, CHUNK, SGU_GROUPS, SGU_GROUP_WIDTH)
    ws = w_spatial * jnp.tril(jnp.ones((CHUNK, CHUNK), w_spatial.dtype))
    sv = jnp.einsum('gts,ncsgw->nctgw', ws, v) + b_spatial.T[None, None, :, :, None]
    sv = sv.reshape(n, tp, SGU_WIDTH)[:, :t]
    return u * sv


def mixer_front(x, g_pre, w_in, g_sgu, w_spatial, b_spatial):
    h = rms_norm(x, g_pre)
    proj = h @ w_in
    cuts = np.cumsum([SGU_WIDTH, SGU_WIDTH, ATTN_WIDTH, ATTN_WIDTH, ATTN_WIDTH, D_MODEL]).tolist()
    u, v, q, k, vv, ga, gb = jnp.split(proj, cuts, axis=-1)
    v_n = rms_norm(v, g_sgu)
    a = spatial_gating(u, v_n, w_spatial, b_spatial)
    n, t = x.shape[0], x.shape[1]
    heads = lambda z: z.reshape(n, t, N_HEADS, HEAD_DIM)
    return a, v_n, heads(q), heads(k), heads(vv), ga, gb


def mixer_back_and_ffn(x, a, att, ga, gb, w_a, w_b, w_o, g_post_mix, g_pre_ffn,
                       w_gate, w_up, w_down, g_post_ffn):
    att = att.reshape(x.shape[0], x.shape[1], ATTN_WIDTH)
    m = jax.nn.sigmoid(ga) * (a @ w_a) + jax.nn.sigmoid(gb) * (att @ w_b)
    x = x + rms_norm(m @ w_o, g_post_mix)
    h = rms_norm(x, g_pre_ffn)
    f = (jax.nn.silu(h @ w_gate) * (h @ w_up)) @ w_down
    return x + rms_norm(f, g_post_ffn)


def select_blocks(gate, q_pos):
    nb = gate.shape[-1]
    bt = q_pos // MOBA_BLOCK
    cand = jnp.arange(nb)[None, :] < bt[:, None]
    g = jnp.where(cand[None, None], gate, -jnp.inf)
    if nb < MOBA_TOPK:
        g = jnp.pad(g, ((0, 0), (0, 0), (0, 0), (0, MOBA_TOPK - nb)), constant_values=-jnp.inf)
    _, idx = lax.top_k(g, MOBA_TOPK)
    idx = jnp.minimum(idx, nb - 1).astype(jnp.int32)
    sel_ok = jnp.arange(MOBA_TOPK)[None, :] < jnp.minimum(bt, MOBA_TOPK)[:, None]
    own = jnp.broadcast_to(bt[None, None, :, None], idx.shape[:3] + (1,)).astype(jnp.int32)
    return jnp.concatenate([idx, own], axis=-1), sel_ok


def key_layout(blk, q_pos, sel_ok):
    n, h, t = blk.shape[0], blk.shape[1], blk.shape[2]
    offs = jnp.arange(MOBA_BLOCK, dtype=jnp.int32)
    key_pos = (blk[..., None] * MOBA_BLOCK + offs).reshape(n, h, t, -1)
    own_start = (q_pos // MOBA_BLOCK) * MOBA_BLOCK
    own_ok = own_start[:, None] + offs[None, :] <= q_pos[:, None]
    sel_rows = jnp.broadcast_to(sel_ok[:, :, None], (t, MOBA_TOPK, MOBA_BLOCK))
    valid = jnp.concatenate([sel_rows, own_ok[:, None, :]], axis=1).reshape(t, -1)
    return key_pos, valid


def moba_core(q, q_pos, k_sel, v_sel, key_pos, valid, slopes):
    s = jnp.einsum('nthd,nhtmd->nhtm', q, k_sel).astype(jnp.float32) * (HEAD_DIM ** -0.5)
    dist = (q_pos[None, None, :, None] - key_pos).astype(jnp.float32)
    s = s - slopes[None, :, None, None] * dist
    s = jnp.where(valid[None, None], s, -jnp.inf)
    p = jax.nn.softmax(s, axis=-1).astype(v_sel.dtype)
    return jnp.einsum('nhtm,nhtmd->nthd', p, v_sel)


def moba_prompt(q, k, v, slopes):
    b, s = q.shape[0], q.shape[1]
    nb = -(-s // MOBA_BLOCK)
    pad = nb * MOBA_BLOCK - s

    def to_blocks(z):
        z = jnp.pad(z, ((0, 0), (0, pad), (0, 0), (0, 0)))
        return z.reshape(b, nb, MOBA_BLOCK, N_HEADS, HEAD_DIM).transpose(0, 3, 1, 2, 4)

    k_blk, v_blk = to_blocks(k), to_blocks(v)
    k_mean = jnp.mean(k_blk.astype(jnp.float32), axis=3)
    b_ar = jnp.arange(b)[:, None, None]
    h_ar = jnp.arange(N_HEADS)[None, :, None]
    nq = s // Q_BLOCK

    def body(args):
        qb, pos = args
        gate = jnp.einsum('bqhd,bhnd->bhqn', qb.astype(jnp.float32), k_mean)
        blk, sel_ok = select_blocks(gate, pos)
        flat = blk.reshape(b, N_HEADS, -1)
        k_sel = k_blk[b_ar, h_ar, flat].reshape(b, N_HEADS, Q_BLOCK, -1, HEAD_DIM)
        v_sel = v_blk[b_ar, h_ar, flat].reshape(b, N_HEADS, Q_BLOCK, -1, HEAD_DIM)
        key_pos, valid = key_layout(blk, pos, sel_ok)
        return moba_core(qb, pos, k_sel, v_sel, key_pos, valid, slopes)

    qs = q.reshape(b, nq, Q_BLOCK, N_HEADS, HEAD_DIM).transpose(1, 0, 2, 3, 4)
    ps = jnp.arange(s, dtype=jnp.int32).reshape(nq, Q_BLOCK)
    out = lax.map(body, (qs, ps))
    return out.transpose(1, 0, 2, 3, 4).reshape(b, s, N_HEADS, HEAD_DIM)


def gather_rows(pool, new, page_table, pos):
    n, t = new.shape[0], new.shape[1]
    n_ar = jnp.arange(n)[:, None, None, None]
    h_ar = jnp.arange(N_HEADS)[None, :, None, None]
    pp = jnp.clip(pos, 0, PAST_LEN - 1)
    phys = page_table[n_ar, pp // PAGE_SIZE]
    past_rows = pool[phys, pp % PAGE_SIZE, h_ar]
    new_rows = new[n_ar, jnp.clip(pos - PAST_LEN, 0, t - 1), h_ar]
    return jnp.where((pos < PAST_LEN)[..., None], past_rows, new_rows)


def moba_sample(q, k_new, v_new, cache_k, cache_v, page_table, slopes):
    n, t = q.shape[0], q.shape[1]
    n_pages = PAST_LEN // PAGE_SIZE
    nb = -(-(PAST_LEN + t) // MOBA_BLOCK)
    page_sums = jnp.sum(cache_k, axis=1, dtype=jnp.float32)
    page_blk = (jnp.arange(n_pages) * PAGE_SIZE) // MOBA_BLOCK
    bsum = jnp.zeros((n, nb, N_HEADS, HEAD_DIM), jnp.float32).at[:, page_blk].add(page_sums[page_table])
    q_pos = PAST_LEN + jnp.arange(t, dtype=jnp.int32)
    bsum = bsum.at[:, q_pos // MOBA_BLOCK].add(k_new.astype(jnp.float32))
    k_mean = bsum / MOBA_BLOCK
    gate = jnp.einsum('nthd,nbhd->nhtb', q.astype(jnp.float32), k_mean)
    blk, sel_ok = select_blocks(gate, q_pos)
    key_pos, valid = key_layout(blk, q_pos, sel_ok)
    k_sel = gather_rows(cache_k, k_new, page_table, key_pos)
    v_sel = gather_rows(cache_v, v_new, page_table, key_pos)
    return moba_core(q, q_pos, k_sel, v_sel, key_pos, valid, slopes)


def setup_inputs(seed: int = 0) -> dict:
    key = jax.random.key(seed)
    ks = jax.random.split(key, 24)
    f32 = jnp.float32
    n_pages = PAST_LEN // PAGE_SIZE
    n_used = DEC_BATCH * n_pages
    n_pool = n_used + max(1, n_used // 4)

    def nrm(k, shape, fan_in):
        return jax.random.normal(k, shape, f32) * (fan_in ** -0.5)

    def gain(k, shape):
        return 1.0 + 0.02 * jax.random.normal(k, shape, f32)

    in_cols = 2 * SGU_WIDTH + 3 * ATTN_WIDTH + 2 * D_MODEL
    page_table = jax.random.permutation(ks[2], n_pool)[:n_used].reshape(DEC_BATCH, n_pages).astype(jnp.int32)
    return {
        'x_prompt': jax.random.normal(ks[0], (BATCH, SEQ, D_MODEL), f32),
        'x_sample': jax.random.normal(ks[1], (DEC_BATCH, DEC_SEQ, D_MODEL), f32),
        'cache_k': jax.random.normal(ks[3], (DEPTH, n_pool, PAGE_SIZE, N_HEADS, HEAD_DIM), f32),
        'cache_v': jax.random.normal(ks[4], (DEPTH, n_pool, PAGE_SIZE, N_HEADS, HEAD_DIM), f32),
        'page_table': page_table,
        'g_pre_mix': gain(ks[5], (DEPTH, D_MODEL)),
        'w_in': nrm(ks[6], (DEPTH, D_MODEL, in_cols), D_MODEL),
        'g_sgu': gain(ks[7], (DEPTH, SGU_WIDTH)),
        'w_spatial': nrm(ks[8], (DEPTH, SGU_GROUPS, CHUNK, CHUNK), CHUNK),
        'b_spatial': 1.0 + 0.1 * jax.random.normal(ks[9], (DEPTH, SGU_GROUPS, CHUNK), f32),
        'w_a': nrm(ks[10], (DEPTH, SGU_WIDTH, D_MODEL), SGU_WIDTH),
        'w_b': nrm(ks[11], (DEPTH, ATTN_WIDTH, D_MODEL), ATTN_WIDTH),
        'w_o': nrm(ks[12], (DEPTH, D_MODEL, D_MODEL), D_MODEL),
        'g_post_mix': gain(ks[13], (DEPTH, D_MODEL)),
        'g_pre_ffn': gain(ks[14], (DEPTH, D_MODEL)),
        'w_gate': nrm(ks[15], (DEPTH, D_MODEL, D_FF), D_MODEL),
        'w_up': nrm(ks[16], (DEPTH, D_MODEL, D_FF), D_MODEL),
        'w_down': nrm(ks[17], (DEPTH, D_FF, D_MODEL), D_FF),
        'g_post_ffn': gain(ks[18], (DEPTH, D_MODEL)),
    }


def reference(x_prompt, x_sample, cache_k, cache_v, page_table, g_pre_mix, w_in, g_sgu,
              w_spatial, b_spatial, w_a, w_b, w_o, g_post_mix, g_pre_ffn, w_gate, w_up,
              w_down, g_post_ffn):
    slopes = alibi_slopes()
    xp, xs = x_prompt, x_sample
    kp_rows, vp_rows, ks_rows, vs_rows, sgu_rows = [], [], [], [], []
    for l in range(DEPTH):
        back = (w_a[l], w_b[l], w_o[l], g_post_mix[l], g_pre_ffn[l], w_gate[l], w_up[l], w_down[l], g_post_ffn[l])
        a, _, q, k, vv, ga, gb = mixer_front(xp, g_pre_mix[l], w_in[l], g_sgu[l], w_spatial[l], b_spatial[l])
        att = moba_prompt(q, k, vv, slopes)
        xp = mixer_back_and_ffn(xp, a, att, ga, gb, *back)
        kp_rows.append(k)
        vp_rows.append(vv)
        a_s, vn_s, q_s, k_s, vv_s, ga_s, gb_s = mixer_front(xs, g_pre_mix[l], w_in[l], g_sgu[l], w_spatial[l], b_spatial[l])
        att_s = moba_sample(q_s, k_s, vv_s, cache_k[l], cache_v[l], page_table, slopes)
        xs = mixer_back_and_ffn(xs, a_s, att_s, ga_s, gb_s, *back)
        ks_rows.append(k_s)
        vs_rows.append(vv_s)
        sgu_rows.append(vn_s)
    return (xp, xs, jnp.stack(kp_rows), jnp.stack(vp_rows), jnp.stack(ks_rows), jnp.stack(vs_rows), jnp.stack(sgu_rows))
```

```python
import functools

import numpy as np
import jax
import jax.numpy as jnp
from jax import lax
from jax.experimental import pallas as pl
from jax.experimental.pallas import tpu as pltpu

F32 = jnp.float32
BF16 = jnp.bfloat16

D_MODEL = 1024
N_HEADS = 8
HEAD_DIM = 64
ATTN_WIDTH = N_HEADS * HEAD_DIM
MOBA_BLOCK = 256
MOBA_TOPK = 3
CHUNK = 128
SGU_GROUPS = 4
SGU_WIDTH = 512
D_FF = 2816
PAGE_SIZE = 128
EPS = 1e-6
ALIBI_MAX_BIAS = 8.0

LANES = 128
PAIR = 2 * HEAD_DIM
N_PAIRS = N_HEADS // 2
NEG_BIG = -1e30
PAGES_PER_BLOCK = MOBA_BLOCK // PAGE_SIZE
SCALE = HEAD_DIM ** -0.5

TILE_FRONT = MOBA_BLOCK
TILE_BACK = 512
FF_CHUNK = 256
SELECT_RING = 4

VMEM_LIMIT_FRONT = 44 * 1024 * 1024
VMEM_LIMIT_ATTN = 32 * 1024 * 1024
VMEM_LIMIT_BACK = 56 * 1024 * 1024
VMEM_LIMIT_DECODE = 32 * 1024 * 1024


def _slopes():
    return [2.0 ** (-(ALIBI_MAX_BIAS / N_HEADS) * (h + 1)) for h in range(N_HEADS)]


def _rms(x, g):
    return x * lax.rsqrt(jnp.mean(x * x, axis=-1, keepdims=True) + EPS) * g


def _split_bf16(x):
    hi = x.astype(BF16)
    lo = (x - hi.astype(F32)).astype(BF16)
    return hi, lo


def _dot_nt_f32(a, b):
    dn = (((1,), (1,)), ((), ()))
    ah, al = _split_bf16(a)
    bh, bl = _split_bf16(b)
    d = functools.partial(lax.dot_general, dimension_numbers=dn, preferred_element_type=F32)
    return d(ah, bh) + d(ah, bl) + d(al, bh)


def _resident(shape):
    nd = len(shape)
    return pl.BlockSpec(shape, lambda *_: (0,) * nd, pipeline_mode=pl.Buffered(1))


def _front_kernel(x_ref, gpre_ref, win_ref, gsgu_ref, wsp_ref, bspt_ref,
                  a_ref, qext_ref, kt_ref, vt_ref, vb_ref, ga_ref, gb_ref, ksum_ref):
    j = pl.program_id(1)

    @pl.when(j == 0)
    def _():
        ksum_ref[...] = jnp.zeros_like(ksum_ref)

    h = _rms(x_ref[0], gpre_ref[...]).astype(BF16)

    def proj(lo, hi):
        return jnp.dot(h, win_ref[:, lo:hi], preferred_element_type=F32)

    c0 = SGU_WIDTH
    c1 = 2 * SGU_WIDTH
    c2 = c1 + ATTN_WIDTH
    c3 = c2 + ATTN_WIDTH
    c4 = c3 + ATTN_WIDTH
    c5 = c4 + D_MODEL
    u = proj(0, c0)
    v = proj(c0, c1)
    q = proj(c1, c2)
    k = proj(c2, c3)
    vv = proj(c3, c4)
    ga_ref[0] = proj(c4, c5)
    gb_ref[0] = proj(c5, c5 + D_MODEL)

    vn = _rms(v, gsgu_ref[...]).astype(BF16)
    tri_r = lax.broadcasted_iota(jnp.int32, (CHUNK, CHUNK), 0)
    tri_c = lax.broadcasted_iota(jnp.int32, (CHUNK, CHUNK), 1)
    for g in range(SGU_GROUPS):
        wg = jnp.where(tri_c <= tri_r, wsp_ref[g], 0.0).astype(BF16)
        cols = slice(g * CHUNK, (g + 1) * CHUNK)
        for c in range(TILE_FRONT // CHUNK):
            rows = slice(c * CHUNK, (c + 1) * CHUNK)
            sv = jnp.dot(wg, vn[rows, cols], preferred_element_type=F32) + bspt_ref[:, g:g + 1]
            a_ref[0, rows, cols] = (u[rows, cols] * sv).astype(BF16)

    kt_ref[0] = k.T
    vt_ref[0] = vv.T
    vb_ref[0] = vv.astype(BF16)

    ksum_ref[pl.ds(j, 1), :] = jnp.sum(k, axis=0, keepdims=True)
    km = ksum_ref[...] * (1.0 / MOBA_BLOCK)
    n_blk = km.shape[0]
    km_rep = jnp.concatenate([km] * N_HEADS + [jnp.zeros((LANES - N_HEADS * n_blk, ATTN_WIDTH), F32)], axis=0)
    r_i = lax.broadcasted_iota(jnp.int32, (LANES, ATTN_WIDTH), 0)
    c_i = lax.broadcasted_iota(jnp.int32, (LANES, ATTN_WIDTH), 1)
    head_rows = (r_i // n_blk == c_i // HEAD_DIM) & (r_i < N_HEADS * n_blk)
    km_bd = jnp.where(head_rows, km_rep, 0.0)
    gate_t = _dot_nt_f32(km_bd, q)

    blk_i = lax.broadcasted_iota(jnp.int32, (n_blk, TILE_FRONT), 0)
    cand = blk_i < j
    bias_rows = []
    for hh in range(N_HEADS):
        gm = jnp.where(cand, gate_t[hh * n_blk:(hh + 1) * n_blk, :], -jnp.inf)
        rank = jnp.zeros((n_blk, TILE_FRONT), F32)
        for m in range(n_blk):
            gmm = gm[m:m + 1, :]
            ahead = (gmm > gm) | ((gmm == gm) & (blk_i > m))
            rank = rank + jnp.where(ahead, 1.0, 0.0)
        keep = (cand & (rank < float(MOBA_TOPK))) | (blk_i == j)
        bias_rows.append(jnp.where(keep, 0.0, NEG_BIG))
    bias_rows.append(jnp.full((LANES - N_HEADS * n_blk, TILE_FRONT), NEG_BIG, F32))
    sel = jnp.concatenate(bias_rows, axis=0).T
    sel_sw = pltpu.roll(sel, LANES // 2, axis=1)

    lane = lax.broadcasted_iota(jnp.int32, (TILE_FRONT, LANES), 1)
    for hh in range(N_HEADS):
        p = hh // 2
        qp = q[:, p * PAIR:(p + 1) * PAIR] * SCALE
        if hh % 2 == 0:
            lo = HEAD_DIM + hh * n_blk
            ext = jnp.where((lane >= lo) & (lane < lo + n_blk), sel_sw,
                            jnp.where(lane >= LANES - 2, 1.0, 0.0))
            qx = jnp.where(lane < HEAD_DIM, qp, ext)
        else:
            lo = hh * n_blk
            ext = jnp.where((lane >= lo) & (lane < lo + n_blk), sel,
                            jnp.where(lane < 2, 1.0, 0.0))
            qx = jnp.where(lane >= HEAD_DIM, qp, ext)
        qext_ref[0, hh] = qx.astype(BF16)


def _key_ext_const(seq):
    n_blk = seq // MOBA_BLOCK
    pos = np.arange(seq)
    off = (pos % MOBA_BLOCK).astype(np.float32)
    blk = pos // MOBA_BLOCK
    out = np.zeros((N_HEADS, PAIR, seq), np.float32)
    for hh, slope in enumerate(_slopes()):
        if hh % 2 == 0:
            base, r_off, r_blk = HEAD_DIM + hh * n_blk, PAIR - 2, PAIR - 1
        else:
            base, r_off, r_blk = hh * n_blk, 0, 1
        for n in range(n_blk):
            out[hh, base + n] = (blk == n)
        out[hh, r_off] = slope * off
        out[hh, r_blk] = slope * MOBA_BLOCK * blk
    return jnp.asarray(out, dtype=BF16)


def _attn_kernel(qext_ref, kt_ref, vb_ref, kext_ref, o_ref, kx_ref):
    qi = pl.program_id(2)
    n_blk = kx_ref.shape[1]

    @pl.when(qi == 0)
    def _():
        kt = kt_ref[0].astype(BF16)
        row = lax.broadcasted_iota(jnp.int32, (PAIR, MOBA_BLOCK), 0)
        for n in range(n_blk):
            cols = slice(n * MOBA_BLOCK, (n + 1) * MOBA_BLOCK)
            kx_ref[0, n] = jnp.where(row < HEAD_DIM, kt[:, cols], kext_ref[0, :, cols])
            kx_ref[1, n] = jnp.where(row >= HEAD_DIM, kt[:, cols], kext_ref[1, :, cols])

    tq = qext_ref.shape[2]
    r_i = lax.broadcasted_iota(jnp.int32, (tq, MOBA_BLOCK), 0)
    c_i = lax.broadcasted_iota(jnp.int32, (tq, MOBA_BLOCK), 1)
    outs = []
    for e in range(2):
        qx = qext_ref[0, e]

        def step(s, vblk, carry):
            m, l, acc = carry
            m_new = jnp.maximum(m, jnp.max(s, axis=-1, keepdims=True))
            alpha = jnp.exp(m - m_new)
            p = jnp.exp(s - m_new)
            l = alpha * l + jnp.sum(p, axis=-1, keepdims=True)
            acc = alpha * acc + jnp.dot(p.astype(BF16), vblk, preferred_element_type=F32)
            return m_new, l, acc

        def body(n, carry):
            s = jnp.dot(qx, kx_ref[e, n], preferred_element_type=F32)
            start = pl.multiple_of(n * MOBA_BLOCK, MOBA_BLOCK)
            return step(s, vb_ref[0, pl.ds(start, MOBA_BLOCK), :], carry)

        init = (jnp.full((tq, 1), -jnp.inf, F32), jnp.zeros((tq, 1), F32), jnp.zeros((tq, PAIR), F32))
        carry = lax.fori_loop(0, qi, body, init)
        s = jnp.dot(qx, kx_ref[e, qi], preferred_element_type=F32)
        s = jnp.where(c_i <= r_i, s, NEG_BIG)
        start = pl.multiple_of(qi * MOBA_BLOCK, MOBA_BLOCK)
        m, l, acc = step(s, vb_ref[0, pl.ds(start, MOBA_BLOCK), :], carry)
        outs.append(acc / l)
    lane = lax.broadcasted_iota(jnp.int32, (tq, PAIR), 1)
    o_ref[0] = jnp.where(lane < HEAD_DIM, outs[0], outs[1]).astype(BF16)


def _back_kernel(x_ref, a_ref, att_ref, ga_ref, gb_ref, wa_ref, wb_ref, wo_ref, gpm_ref, gpf_ref,
                 wg_ref, wu_ref, wd_ref, gpo_ref, o_ref):
    ya = jnp.dot(a_ref[...], wa_ref[...], preferred_element_type=F32)
    yb = jnp.dot(att_ref[...], wb_ref[...], preferred_element_type=F32)
    m = jax.nn.sigmoid(ga_ref[...]) * ya + jax.nn.sigmoid(gb_ref[...]) * yb
    y = jnp.dot(m.astype(BF16), wo_ref[...], preferred_element_type=F32)
    x1 = x_ref[...] + _rms(y, gpm_ref[...])
    h2 = _rms(x1, gpf_ref[...]).astype(BF16)
    acc = jnp.zeros(x1.shape, F32)
    for c in range(D_FF // FF_CHUNK):
        cols = slice(c * FF_CHUNK, (c + 1) * FF_CHUNK)
        g = jnp.dot(h2, wg_ref[:, cols], preferred_element_type=F32)
        up = jnp.dot(h2, wu_ref[:, cols], preferred_element_type=F32)
        act = (g * jax.nn.sigmoid(g)) * up
        acc = acc + jnp.dot(act.astype(BF16), wd_ref[cols, :], preferred_element_type=F32)
    o_ref[...] = x1 + _rms(acc, gpo_ref[...])


def _back(x, a, att, ga, gb, w_a, w_b, w_o, gpm, gpf, w_g, w_u, w_d, gpo):
    n = x.shape[0]
    tm = min(TILE_BACK, n)
    row = lambda w: pl.BlockSpec((tm, w), lambda i: (i, 0))
    return pl.pallas_call(
        _back_kernel,
        grid=(n // tm,),
        in_specs=[row(D_MODEL), row(SGU_WIDTH), row(ATTN_WIDTH), row(D_MODEL), row(D_MODEL),
                  _resident(w_a.shape), _resident(w_b.shape), _resident(w_o.shape),
                  _resident(gpm.shape), _resident(gpf.shape),
                  _resident(w_g.shape), _resident(w_u.shape), _resident(w_d.shape), _resident(gpo.shape)],
        out_specs=row(D_MODEL),
        out_shape=jax.ShapeDtypeStruct((n, D_MODEL), F32),
        compiler_params=pltpu.CompilerParams(dimension_semantics=("arbitrary",),
                                             vmem_limit_bytes=VMEM_LIMIT_BACK),
        name="mixer_back_ffn",
    )(x, a, att, ga, gb, w_a, w_b, w_o, gpm, gpf, w_g, w_u, w_d, gpo)


def _front_sample_kernel(x_ref, gpre_ref, win_ref, gsgu_ref, coef_ref, bias_ref,
                         a_ref, vn_ref, q_ref, k_ref, v_ref, ga_ref, gb_ref):
    h = _rms(x_ref[...], gpre_ref[...]).astype(BF16)

    def proj(lo, hi):
        return jnp.dot(h, win_ref[:, lo:hi], preferred_element_type=F32)

    c0 = SGU_WIDTH
    c1 = 2 * SGU_WIDTH
    c2 = c1 + ATTN_WIDTH
    c3 = c2 + ATTN_WIDTH
    c4 = c3 + ATTN_WIDTH
    c5 = c4 + D_MODEL
    u = proj(0, c0)
    vn = _rms(proj(c0, c1), gsgu_ref[...])
    vn_ref[...] = vn
    a_ref[...] = (u * (vn * coef_ref[...] + bias_ref[...])).astype(BF16)
    q_ref[...] = proj(c1, c2)
    k_ref[...] = proj(c2, c3)
    v_ref[...] = proj(c3, c4)
    ga_ref[...] = proj(c4, c5)
    gb_ref[...] = proj(c5, c5 + D_MODEL)


def _lane_bcast_col(row, n_rows):
    return jnp.broadcast_to(row, (n_rows, row.shape[1])).T


def _head_sums(row):
    return [jnp.sum(row[:, h * HEAD_DIM:(h + 1) * HEAD_DIM], axis=1, keepdims=True) for h in range(N_HEADS)]


def _select_kernel(pt_ref, q_ref, k_ref, ck_ref, idx_ref, buf_ref, sem_ref, *, n_samples, n_pages, n_past_blk):
    n = pl.program_id(0)
    per_step = n_past_blk // SELECT_RING

    def copies(sample, blk, slot):
        base = sample * n_pages + blk * PAGES_PER_BLOCK
        return [pltpu.make_async_copy(ck_ref.at[pt_ref[base + pg]], buf_ref.at[slot, pg], sem_ref.at[slot])
                for pg in range(PAGES_PER_BLOCK)]

    @pl.when(n == 0)
    def _():
        for slot in range(SELECT_RING):
            for cp in copies(0, slot, slot):
                cp.start()

    q_row = q_ref[pl.ds(n, 1), :]
    qb = _lane_bcast_col(q_row, PAGE_SIZE)
    lane = lax.broadcasted_iota(jnp.int32, (N_HEADS, LANES), 1)
    sub = lax.broadcasted_iota(jnp.int32, (N_HEADS, LANES), 0)

    def outer(g, gate):
        for slot in range(SELECT_RING):
            blk = g * SELECT_RING + slot
            for cp in copies(n, blk, slot):
                cp.wait()
            t = (buf_ref[slot, 0] + buf_ref[slot, 1]) * qb
            s_blk = jnp.sum(t.reshape(N_HEADS, HEAD_DIM, PAGE_SIZE), axis=1)
            gsum = jnp.sum(s_blk, axis=1, keepdims=True)
            gate = jnp.where(lane == blk, gsum, gate)
            nxt = blk + SELECT_RING
            wrap = nxt >= n_past_blk
            nsample = jnp.where(wrap, n + 1, n)
            nblk = jnp.where(wrap, nxt - n_past_blk, nxt)

            @pl.when(nsample < n_samples)
            def _():
                for cp in copies(nsample, nblk, slot):
                    cp.start()
        return gate

    gate = lax.fori_loop(0, per_step, outer, jnp.zeros((N_HEADS, LANES), F32))
    own = _head_sums(q_row * k_ref[pl.ds(n, 1), :])
    for h in range(N_HEADS):
        gate = jnp.where((lane == n_past_blk) & (sub == h), own[h], gate)
    gate = gate * (1.0 / MOBA_BLOCK)
    gm = jnp.where(lane < n_past_blk, gate, -jnp.inf)
    lane_f = lane.astype(F32)
    out = jnp.zeros((N_HEADS, LANES), F32)
    for r in range(MOBA_TOPK):
        mx = jnp.max(gm, axis=1, keepdims=True)
        pick = jnp.min(jnp.where(gm == mx, lane_f, float(LANES)), axis=1, keepdims=True)
        out = jnp.where(lane == r, pick, out)
        gm = jnp.where(lane_f == pick, -jnp.inf, gm)
    idx_ref[0] = out.astype(jnp.int32)


def _decode_attn_kernel(pt_ref, idx_ref, q_ref, k_ref, v_ref, ck_ref, cv_ref, o_ref,
                        kbuf_ref, vbuf_ref, ksem_ref, vsem_ref, *, n_samples, n_pages, past_len):
    n = pl.program_id(0)
    n_tiles = MOBA_TOPK * PAGES_PER_BLOCK
    per_sample = N_HEADS * n_tiles

    def copies(sample, slot):
        out = []
        for h in range(N_HEADS):
            for j in range(MOBA_TOPK):
                blk = idx_ref[sample * (N_HEADS * MOBA_TOPK) + h * MOBA_TOPK + j]
                for pg in range(PAGES_PER_BLOCK):
                    phys = pt_ref[sample * n_pages + blk * PAGES_PER_BLOCK + pg]
                    i = (h * MOBA_TOPK + j) * PAGES_PER_BLOCK + pg
                    out.append(pltpu.make_async_copy(ck_ref.at[phys, h], kbuf_ref.at[slot, i], ksem_ref.at[slot]))
                    out.append(pltpu.make_async_copy(cv_ref.at[phys, h], vbuf_ref.at[slot, i], vsem_ref.at[slot]))
        return out

    @pl.when(n == 0)
    def _():
        for cp in copies(0, 0):
            cp.start()

    @pl.when(n + 1 < n_samples)
    def _():
        for cp in copies(n + 1, (n + 1) % 2):
            cp.start()

    slot = n % 2
    for i in range(per_sample):
        pltpu.make_async_copy(ck_ref.at[0, 0], kbuf_ref.at[slot, i], ksem_ref.at[slot]).wait()
        pltpu.make_async_copy(cv_ref.at[0, 0], vbuf_ref.at[slot, i], vsem_ref.at[slot]).wait()

    q_row = q_ref[pl.ds(n, 1), :]
    k_row = k_ref[pl.ds(n, 1), :]
    v_row = v_ref[pl.ds(n, 1), :]
    qb = _lane_bcast_col(q_row, PAGE_SIZE)
    s_new = _head_sums(q_row * k_row)
    lane_row = lax.broadcasted_iota(jnp.int32, (1, PAGE_SIZE), 1)
    lane_w = lax.broadcasted_iota(jnp.int32, (1, ATTN_WIDTH), 1)
    acc_heads = []
    p_new_row = jnp.zeros((1, ATTN_WIDTH), F32)
    for h, slope in enumerate(_slopes()):
        qh = qb[h * HEAD_DIM:(h + 1) * HEAD_DIM, :]
        scores = []
        for j in range(MOBA_TOPK):
            blk = idx_ref[n * (N_HEADS * MOBA_TOPK) + h * MOBA_TOPK + j]
            for pg in range(PAGES_PER_BLOCK):
                i = (h * MOBA_TOPK + j) * PAGES_PER_BLOCK + pg
                s = jnp.sum(kbuf_ref[slot, i] * qh, axis=0, keepdims=True) * SCALE
                key_pos = blk * MOBA_BLOCK + pg * PAGE_SIZE + lane_row
                scores.append(s - slope * (past_len - key_pos).astype(F32))
        s_n = s_new[h] * SCALE
        m = s_n
        for s in scores:
            m = jnp.maximum(m, jnp.max(s, axis=1, keepdims=True))
        p_n = jnp.exp(s_n - m)
        l = p_n
        acc = jnp.zeros((HEAD_DIM, PAGE_SIZE), F32)
        for t, s in enumerate(scores):
            p = jnp.exp(s - m)
            l = l + jnp.sum(p, axis=1, keepdims=True)
            acc = acc + vbuf_ref[slot, h * n_tiles + t] * p
        acc_heads.append(acc / l)
        p_new_row = jnp.where(lane_w // HEAD_DIM == h, p_n / l, p_new_row)
    acc_all = jnp.concatenate(acc_heads, axis=0)
    o_row = jnp.sum(acc_all.T, axis=0, keepdims=True) + p_new_row * v_row
    o_ref[0] = o_row


def kernel(x_prompt, x_sample, cache_k, cache_v, page_table, g_pre_mix, w_in, g_sgu, w_spatial, b_spatial,
           w_a, w_b, w_o, g_post_mix, g_pre_ffn, w_gate, w_up, w_down, g_post_ffn):
    depth = w_in.shape[0]
    assert depth == 1, "kernels are written for a single layer"
    bsz, seq, _ = x_prompt.shape
    n_samples, dec_seq, _ = x_sample.shape
    assert dec_seq == 1
    n_pool = cache_k.shape[1]
    n_pages = page_table.shape[1]
    past_len = n_pages * PAGE_SIZE
    n_past_blk = past_len // MOBA_BLOCK
    n_blk = seq // MOBA_BLOCK
    assert n_blk * N_HEADS <= HEAD_DIM and n_past_blk < LANES and n_past_blk % SELECT_RING == 0

    l = 0
    w_in_b = w_in[l].astype(BF16)
    back_w = (w_a[l].astype(BF16), w_b[l].astype(BF16), w_o[l].astype(BF16), g_post_mix[l][None], g_pre_ffn[l][None],
              w_gate[l].astype(BF16), w_up[l].astype(BF16), w_down[l].astype(BF16), g_post_ffn[l][None])
    gpre = g_pre_mix[l][None]
    gsgu = g_sgu[l][None]
    in_cols = w_in_b.shape[1]

    n_tiles = seq // TILE_FRONT
    tile3 = lambda w: pl.BlockSpec((1, TILE_FRONT, w), lambda b, j: (b, j, 0))
    tile3t = lambda w: pl.BlockSpec((1, w, TILE_FRONT), lambda b, j: (b, 0, j))
    a_p, qext, kt, vt, vb, ga, gb = pl.pallas_call(
        _front_kernel,
        grid=(bsz, n_tiles),
        in_specs=[tile3(D_MODEL), _resident((1, D_MODEL)), _resident((D_MODEL, in_cols)), _resident((1, SGU_WIDTH)),
                  _resident((SGU_GROUPS, CHUNK, CHUNK)), _resident((CHUNK, SGU_GROUPS))],
        out_specs=[tile3(SGU_WIDTH),
                   pl.BlockSpec((1, N_HEADS, TILE_FRONT, PAIR), lambda b, j: (b, 0, j, 0)),
                   tile3t(ATTN_WIDTH), tile3t(ATTN_WIDTH), tile3(ATTN_WIDTH), tile3(D_MODEL), tile3(D_MODEL)],
        out_shape=[jax.ShapeDtypeStruct((bsz, seq, SGU_WIDTH), BF16),
                   jax.ShapeDtypeStruct((bsz, N_HEADS, seq, PAIR), BF16),
                   jax.ShapeDtypeStruct((bsz, ATTN_WIDTH, seq), F32),
                   jax.ShapeDtypeStruct((bsz, ATTN_WIDTH, seq), F32),
                   jax.ShapeDtypeStruct((bsz, seq, ATTN_WIDTH), BF16),
                   jax.ShapeDtypeStruct((bsz, seq, D_MODEL), F32),
                   jax.ShapeDtypeStruct((bsz, seq, D_MODEL), F32)],
        scratch_shapes=[pltpu.VMEM((n_blk, ATTN_WIDTH), F32)],
        compiler_params=pltpu.CompilerParams(dimension_semantics=("arbitrary", "arbitrary"),
                                             vmem_limit_bytes=VMEM_LIMIT_FRONT),
        name="prompt_front",
    )(x_prompt, gpre, w_in_b, gsgu, w_spatial[l], b_spatial[l].T)

    kext = _key_ext_const(seq)
    att_p = pl.pallas_call(
        _attn_kernel,
        grid=(bsz, N_PAIRS, n_tiles),
        in_specs=[pl.BlockSpec((1, 2, TILE_FRONT, PAIR), lambda b, p, i: (b, p, i, 0)),
                  pl.BlockSpec((1, PAIR, seq), lambda b, p, i: (b, p, 0)),
                  pl.BlockSpec((1, seq, PAIR), lambda b, p, i: (b, 0, p)),
                  pl.BlockSpec((2, PAIR, seq), lambda b, p, i: (p, 0, 0))],
        out_specs=pl.BlockSpec((1, TILE_FRONT, PAIR), lambda b, p, i: (b, i, p)),
        out_shape=jax.ShapeDtypeStruct((bsz, seq, ATTN_WIDTH), BF16),
        scratch_shapes=[pltpu.VMEM((2, n_blk, PAIR, MOBA_BLOCK), BF16)],
        compiler_params=pltpu.CompilerParams(dimension_semantics=("arbitrary", "arbitrary", "arbitrary"),
                                             vmem_limit_bytes=VMEM_LIMIT_ATTN),
        name="prompt_moba",
    )(qext, kt, vb, kext)

    n_tok = bsz * seq
    flat = lambda z: z.reshape(n_tok, z.shape[-1])
    y_prompt = _back(flat(x_prompt), flat(a_p), flat(att_p), flat(ga), flat(gb), *back_w).reshape(bsz, seq, D_MODEL)

    xs = x_sample.reshape(n_samples, D_MODEL)
    coef = jnp.repeat(w_spatial[l, :, 0, 0], CHUNK)[None]
    bias = jnp.repeat(b_spatial[l, :, 0], CHUNK)[None]
    full = lambda shape: pl.BlockSpec(shape, lambda *_: (0,) * len(shape))
    s_w = lambda w, dt: jax.ShapeDtypeStruct((n_samples, w), dt)
    a_s, vn_s, q_s, k_s, v_s, ga_s, gb_s = pl.pallas_call(
        _front_sample_kernel,
        grid=(1,),
        in_specs=[full((n_samples, D_MODEL)), full((1, D_MODEL)), full((D_MODEL, in_cols)), full((1, SGU_WIDTH)),
                  full((1, SGU_WIDTH)), full((1, SGU_WIDTH))],
        out_specs=[full((n_samples, SGU_WIDTH)), full((n_samples, SGU_WIDTH)), full((n_samples, ATTN_WIDTH)),
                   full((n_samples, ATTN_WIDTH)), full((n_samples, ATTN_WIDTH)),
                   full((n_samples, D_MODEL)), full((n_samples, D_MODEL))],
        out_shape=[s_w(SGU_WIDTH, BF16), s_w(SGU_WIDTH, F32), s_w(ATTN_WIDTH, F32), s_w(ATTN_WIDTH, F32),
                   s_w(ATTN_WIDTH, F32), s_w(D_MODEL, F32), s_w(D_MODEL, F32)],
        compiler_params=pltpu.CompilerParams(dimension_semantics=("arbitrary",),
                                             vmem_limit_bytes=VMEM_LIMIT_FRONT),
        name="sample_front",
    )(xs, gpre, w_in_b, gsgu, coef, bias)

    ck_t = jnp.transpose(cache_k[l], (0, 2, 3, 1))
    cv_t = jnp.transpose(cache_v[l], (0, 2, 3, 1))
    pt_flat = page_table.reshape(-1)

    idx = pl.pallas_call(
        functools.partial(_select_kernel, n_samples=n_samples, n_pages=n_pages, n_past_blk=n_past_blk),
        grid_spec=pltpu.PrefetchScalarGridSpec(
            num_scalar_prefetch=1,
            grid=(n_samples,),
            in_specs=[pl.BlockSpec((n_samples, ATTN_WIDTH), lambda i, pt: (0, 0)),
                      pl.BlockSpec((n_samples, ATTN_WIDTH), lambda i, pt: (0, 0)),
                      pl.BlockSpec(memory_space=pl.ANY)],
            out_specs=pl.BlockSpec((1, N_HEADS, LANES), lambda i, pt: (i, 0, 0)),
            scratch_shapes=[pltpu.VMEM((SELECT_RING, PAGES_PER_BLOCK, ATTN_WIDTH, PAGE_SIZE), F32),
                            pltpu.SemaphoreType.DMA((SELECT_RING,))]),
        out_shape=jax.ShapeDtypeStruct((n_samples, N_HEADS, LANES), jnp.int32),
        compiler_params=pltpu.CompilerParams(dimension_semantics=("arbitrary",),
                                             vmem_limit_bytes=VMEM_LIMIT_DECODE),
        name="sample_block_select",
    )(pt_flat, q_s, k_s, ck_t.reshape(n_pool, ATTN_WIDTH, PAGE_SIZE))
    idx_flat = idx[:, :, :MOBA_TOPK].reshape(-1)

    n_gather = N_HEADS * MOBA_TOPK * PAGES_PER_BLOCK
    att_s = pl.pallas_call(
        functools.partial(_decode_attn_kernel, n_samples=n_samples, n_pages=n_pages, past_len=past_len),
        grid_spec=pltpu.PrefetchScalarGridSpec(
            num_scalar_prefetch=2,
            grid=(n_samples,),
            in_specs=[pl.BlockSpec((n_samples, ATTN_WIDTH), lambda i, pt, ix: (0, 0)),
                      pl.BlockSpec((n_samples, ATTN_WIDTH), lambda i, pt, ix: (0, 0)),
                      pl.BlockSpec((n_samples, ATTN_WIDTH), lambda i, pt, ix: (0, 0)),
                      pl.BlockSpec(memory_space=pl.ANY), pl.BlockSpec(memory_space=pl.ANY)],
            out_specs=pl.BlockSpec((1, 1, ATTN_WIDTH), lambda i, pt, ix: (i, 0, 0)),
            scratch_shapes=[pltpu.VMEM((2, n_gather, HEAD_DIM, PAGE_SIZE), F32),
                            pltpu.VMEM((2, n_gather, HEAD_DIM, PAGE_SIZE), F32),
                            pltpu.SemaphoreType.DMA((2,)), pltpu.SemaphoreType.DMA((2,))]),
        out_shape=jax.ShapeDtypeStruct((n_samples, 1, ATTN_WIDTH), F32),
        compiler_params=pltpu.CompilerParams(dimension_semantics=("arbitrary",),
                                             vmem_limit_bytes=VMEM_LIMIT_DECODE),
        name="sample_moba",
    )(pt_flat, idx_flat, q_s, k_s, v_s, ck_t, cv_t)
    att_s = att_s.reshape(n_samples, ATTN_WIDTH).astype(BF16)

    y_sample = _back(xs, a_s, att_s, ga_s, gb_s, *back_w).reshape(n_samples, 1, D_MODEL)

    heads_t = lambda zt: jnp.transpose(zt.reshape(1, bsz, N_HEADS, HEAD_DIM, seq), (0, 1, 4, 2, 3))
    heads_s = lambda z: z.reshape(1, n_samples, 1, N_HEADS, HEAD_DIM)
    return (y_prompt, y_sample, heads_t(kt), heads_t(vt), heads_s(k_s), heads_s(v_s),
            vn_s.reshape(1, n_samples, 1, SGU_WIDTH))
```

```python
import functools

import numpy as np
import jax
import jax.numpy as jnp
from jax import lax
from jax.experimental import pallas as pl
from jax.experimental.pallas import tpu as pltpu

F32 = jnp.float32
BF16 = jnp.bfloat16

D_MODEL = 1024
N_HEADS = 8
HEAD_DIM = 64
ATTN_WIDTH = N_HEADS * HEAD_DIM
MOBA_BLOCK = 256
MOBA_TOPK = 3
CHUNK = 128
SGU_GROUPS = 4
SGU_WIDTH = 512
D_FF = 2816
PAGE_SIZE = 128
EPS = 1e-6
ALIBI_MAX_BIAS = 8.0

LANES = 128
PAIR = 2 * HEAD_DIM
N_PAIRS = N_HEADS // 2
NEG_BIG = -1e30
PAGES_PER_BLOCK = MOBA_BLOCK // PAGE_SIZE
SCALE = HEAD_DIM ** -0.5

TILE_FRONT = MOBA_BLOCK
TILE_BACK = 512
FF_CHUNK = 256
SELECT_RING = 16
SELECT_GROUP = 4

VMEM_LIMIT_FRONT = 44 * 1024 * 1024
VMEM_LIMIT_ATTN = 48 * 1024 * 1024
VMEM_LIMIT_BACK = 56 * 1024 * 1024
VMEM_LIMIT_DECODE = 32 * 1024 * 1024


def _slopes():
    return [2.0 ** (-(ALIBI_MAX_BIAS / N_HEADS) * (h + 1)) for h in range(N_HEADS)]


def _rms(x, g):
    return x * lax.rsqrt(jnp.mean(x * x, axis=-1, keepdims=True) + EPS) * g


def _split_bf16(x):
    hi = x.astype(BF16)
    lo = (x - hi.astype(F32)).astype(BF16)
    return hi, lo


def _dot_nt_f32(a, b):
    dn = (((1,), (1,)), ((), ()))
    ah, al = _split_bf16(a)
    bh, bl = _split_bf16(b)
    d = functools.partial(lax.dot_general, dimension_numbers=dn, preferred_element_type=F32)
    return d(ah, bh) + d(ah, bl) + d(al, bh)


def _resident(shape):
    nd = len(shape)
    return pl.BlockSpec(shape, lambda *_: (0,) * nd, pipeline_mode=pl.Buffered(1))


def _front_kernel(x_ref, gpre_ref, win_ref, gsgu_ref, wsp_ref, bspt_ref,
                  a_ref, qext_ref, kt_ref, vt_ref, vb_ref, ga_ref, gb_ref, ksum_ref):
    j = pl.program_id(1)

    @pl.when(j == 0)
    def _():
        ksum_ref[...] = jnp.zeros_like(ksum_ref)

    h = _rms(x_ref[0], gpre_ref[...]).astype(BF16)

    def proj(lo, hi):
        return jnp.dot(h, win_ref[:, lo:hi], preferred_element_type=F32)

    c0 = SGU_WIDTH
    c1 = 2 * SGU_WIDTH
    c2 = c1 + ATTN_WIDTH
    c3 = c2 + ATTN_WIDTH
    c4 = c3 + ATTN_WIDTH
    c5 = c4 + D_MODEL
    u = proj(0, c0)
    v = proj(c0, c1)
    q = proj(c1, c2)
    k = proj(c2, c3)
    vv = proj(c3, c4)
    ga_ref[0] = proj(c4, c5)
    gb_ref[0] = proj(c5, c5 + D_MODEL)

    vn = _rms(v, gsgu_ref[...]).astype(BF16)
    tri_r = lax.broadcasted_iota(jnp.int32, (CHUNK, CHUNK), 0)
    tri_c = lax.broadcasted_iota(jnp.int32, (CHUNK, CHUNK), 1)
    for g in range(SGU_GROUPS):
        wg = jnp.where(tri_c <= tri_r, wsp_ref[g], 0.0).astype(BF16)
        cols = slice(g * CHUNK, (g + 1) * CHUNK)
        for c in range(TILE_FRONT // CHUNK):
            rows = slice(c * CHUNK, (c + 1) * CHUNK)
            sv = jnp.dot(wg, vn[rows, cols], preferred_element_type=F32) + bspt_ref[:, g:g + 1]
            a_ref[0, rows, cols] = (u[rows, cols] * sv).astype(BF16)

    kt_ref[0] = k.T
    vt_ref[0] = vv.T
    vb_ref[0] = vv.astype(BF16)

    ksum_ref[pl.ds(j, 1), :] = jnp.sum(k, axis=0, keepdims=True)
    km = ksum_ref[...] * (1.0 / MOBA_BLOCK)
    n_blk = km.shape[0]
    km_rep = jnp.concatenate([km] * N_HEADS + [jnp.zeros((LANES - N_HEADS * n_blk, ATTN_WIDTH), F32)], axis=0)
    r_i = lax.broadcasted_iota(jnp.int32, (LANES, ATTN_WIDTH), 0)
    c_i = lax.broadcasted_iota(jnp.int32, (LANES, ATTN_WIDTH), 1)
    head_rows = (r_i // n_blk == c_i // HEAD_DIM) & (r_i < N_HEADS * n_blk)
    km_bd = jnp.where(head_rows, km_rep, 0.0)
    gate_t = _dot_nt_f32(km_bd, q)

    blk_i = lax.broadcasted_iota(jnp.int32, (n_blk, TILE_FRONT), 0)
    cand = blk_i < j
    bias_rows = []
    for hh in range(N_HEADS):
        gm = jnp.where(cand, gate_t[hh * n_blk:(hh + 1) * n_blk, :], -jnp.inf)
        rank = jnp.zeros((n_blk, TILE_FRONT), F32)
        for m in range(n_blk):
            gmm = gm[m:m + 1, :]
            ahead = (gmm > gm) | ((gmm == gm) & (blk_i > m))
            rank = rank + jnp.where(ahead, 1.0, 0.0)
        keep = (cand & (rank < float(MOBA_TOPK))) | (blk_i == j)
        bias_rows.append(jnp.where(keep, 0.0, NEG_BIG))
    bias_rows.append(jnp.full((LANES - N_HEADS * n_blk, TILE_FRONT), NEG_BIG, F32))
    sel = jnp.concatenate(bias_rows, axis=0).T
    sel_sw = pltpu.roll(sel, LANES // 2, axis=1)

    lane = lax.broadcasted_iota(jnp.int32, (TILE_FRONT, LANES), 1)
    for hh in range(N_HEADS):
        p = hh // 2
        qp = q[:, p * PAIR:(p + 1) * PAIR] * SCALE
        if hh % 2 == 0:
            lo = HEAD_DIM + hh * n_blk
            ext = jnp.where((lane >= lo) & (lane < lo + n_blk), sel_sw,
                            jnp.where(lane >= LANES - 2, 1.0, 0.0))
            qx = jnp.where(lane < HEAD_DIM, qp, ext)
        else:
            lo = hh * n_blk
            ext = jnp.where((lane >= lo) & (lane < lo + n_blk), sel,
                            jnp.where(lane < 2, 1.0, 0.0))
            qx = jnp.where(lane >= HEAD_DIM, qp, ext)
        qext_ref[0, hh] = qx.astype(BF16)


def _key_ext_const(seq):
    n_blk = seq // MOBA_BLOCK
    pos = np.arange(seq)
    off = (pos % MOBA_BLOCK).astype(np.float32)
    blk = pos // MOBA_BLOCK
    out = np.zeros((N_HEADS, PAIR, seq), np.float32)
    for hh, slope in enumerate(_slopes()):
        if hh % 2 == 0:
            base, r_off, r_blk = HEAD_DIM + hh * n_blk, PAIR - 2, PAIR - 1
        else:
            base, r_off, r_blk = hh * n_blk, 0, 1
        for n in range(n_blk):
            out[hh, base + n] = (blk == n)
        out[hh, r_off] = slope * off
        out[hh, r_blk] = slope * MOBA_BLOCK * blk
    return jnp.asarray(out, dtype=BF16)


def _attn_kernel(qext_ref, kt_ref, vb_ref, kext_ref, o_ref, kx_ref):
    seq = kt_ref.shape[2]
    n_blk = seq // MOBA_BLOCK
    kt = kt_ref[0].astype(BF16)
    row = lax.broadcasted_iota(jnp.int32, (PAIR, seq), 0)
    kx_ref[0] = jnp.where(row < HEAD_DIM, kt, kext_ref[0])
    kx_ref[1] = jnp.where(row >= HEAD_DIM, kt, kext_ref[1])

    r_i = lax.broadcasted_iota(jnp.int32, (MOBA_BLOCK, MOBA_BLOCK), 0)
    c_i = lax.broadcasted_iota(jnp.int32, (MOBA_BLOCK, MOBA_BLOCK), 1)
    causal = c_i <= r_i
    lane = lax.broadcasted_iota(jnp.int32, (MOBA_BLOCK, PAIR), 1)
    for j in range(n_blk):
        w = (j + 1) * MOBA_BLOCK
        rows = slice(j * MOBA_BLOCK, w)
        outs = []
        for e in range(2):
            s = jnp.dot(qext_ref[0, e, rows, :], kx_ref[e, :, :w], preferred_element_type=F32)
            own = jnp.where(causal, s[:, w - MOBA_BLOCK:], NEG_BIG)
            s = own if j == 0 else jnp.concatenate([s[:, :w - MOBA_BLOCK], own], axis=1)
            m = jnp.max(s, axis=-1, keepdims=True)
            p = jnp.exp(s - m)
            l = jnp.sum(p, axis=-1, keepdims=True)
            o = jnp.dot(p.astype(BF16), vb_ref[0, :w, :], preferred_element_type=F32)
            outs.append(o / l)
        o_ref[0, rows, :] = jnp.where(lane < HEAD_DIM, outs[0], outs[1]).astype(BF16)


def _back_kernel(x_ref, a_ref, att_ref, ga_ref, gb_ref, wa_ref, wb_ref, wo_ref, gpm_ref, gpf_ref,
                 wg_ref, wu_ref, wd_ref, gpo_ref, o_ref):
    ya = jnp.dot(a_ref[...], wa_ref[...], preferred_element_type=F32)
    yb = jnp.dot(att_ref[...], wb_ref[...], preferred_element_type=F32)
    m = jax.nn.sigmoid(ga_ref[...]) * ya + jax.nn.sigmoid(gb_ref[...]) * yb
    y = jnp.dot(m.astype(BF16), wo_ref[...], preferred_element_type=F32)
    x1 = x_ref[...] + _rms(y, gpm_ref[...])
    h2 = _rms(x1, gpf_ref[...]).astype(BF16)
    acc = jnp.zeros(x1.shape, F32)
    for c in range(D_FF // FF_CHUNK):
        cols = slice(c * FF_CHUNK, (c + 1) * FF_CHUNK)
        g = jnp.dot(h2, wg_ref[:, cols], preferred_element_type=F32)
        up = jnp.dot(h2, wu_ref[:, cols], preferred_element_type=F32)
        act = (g * jax.nn.sigmoid(g)) * up
        acc = acc + jnp.dot(act.astype(BF16), wd_ref[cols, :], preferred_element_type=F32)
    o_ref[...] = x1 + _rms(acc, gpo_ref[...])


def _back(x, a, att, ga, gb, w_a, w_b, w_o, gpm, gpf, w_g, w_u, w_d, gpo):
    n = x.shape[0]
    tm = min(TILE_BACK, n)
    row = lambda w: pl.BlockSpec((tm, w), lambda i: (i, 0))
    return pl.pallas_call(
        _back_kernel,
        grid=(n // tm,),
        in_specs=[row(D_MODEL), row(SGU_WIDTH), row(ATTN_WIDTH), row(D_MODEL), row(D_MODEL),
                  _resident(w_a.shape), _resident(w_b.shape), _resident(w_o.shape),
                  _resident(gpm.shape), _resident(gpf.shape),
                  _resident(w_g.shape), _resident(w_u.shape), _resident(w_d.shape), _resident(gpo.shape)],
        out_specs=row(D_MODEL),
        out_shape=jax.ShapeDtypeStruct((n, D_MODEL), F32),
        compiler_params=pltpu.CompilerParams(dimension_semantics=("arbitrary",),
                                             vmem_limit_bytes=VMEM_LIMIT_BACK),
        name="mixer_back_ffn",
    )(x, a, att, ga, gb, w_a, w_b, w_o, gpm, gpf, w_g, w_u, w_d, gpo)


def _front_sample_kernel(x_ref, gpre_ref, win_ref, gsgu_ref, coef_ref, bias_ref,
                         a_ref, vn_ref, q_ref, k_ref, v_ref, ga_ref, gb_ref):
    h = _rms(x_ref[...], gpre_ref[...]).astype(BF16)

    def proj(lo, hi):
        return jnp.dot(h, win_ref[:, lo:hi], preferred_element_type=F32)

    c0 = SGU_WIDTH
    c1 = 2 * SGU_WIDTH
    c2 = c1 + ATTN_WIDTH
    c3 = c2 + ATTN_WIDTH
    c4 = c3 + ATTN_WIDTH
    c5 = c4 + D_MODEL
    u = proj(0, c0)
    vn = _rms(proj(c0, c1), gsgu_ref[...])
    vn_ref[...] = vn
    a_ref[...] = (u * (vn * coef_ref[...] + bias_ref[...])).astype(BF16)
    q_ref[...] = proj(c1, c2)
    k_ref[...] = proj(c2, c3)
    v_ref[...] = proj(c3, c4)
    ga_ref[...] = proj(c4, c5)
    gb_ref[...] = proj(c5, c5 + D_MODEL)


def _lane_bcast_col(row, n_rows):
    return jnp.broadcast_to(row, (n_rows, row.shape[1])).T


def _head_sums(row):
    return [jnp.sum(row[:, h * HEAD_DIM:(h + 1) * HEAD_DIM], axis=1, keepdims=True) for h in range(N_HEADS)]


def _select_kernel(pt_ref, q_ref, k_ref, ck_ref, idx_ref, buf_ref, sem_ref, *, n_samples, n_pages, n_past_blk):
    n = pl.program_id(0)

    def copies(sample, blk, slot):
        base = sample * n_pages + blk * PAGES_PER_BLOCK
        return [pltpu.make_async_copy(ck_ref.at[pt_ref[base + pg]], buf_ref.at[slot, pg], sem_ref.at[slot])
                for pg in range(PAGES_PER_BLOCK)]

    @pl.when(n == 0)
    def _():
        for slot in range(SELECT_RING):
            for cp in copies(0, slot, slot):
                cp.start()

    q_row = q_ref[pl.ds(n, 1), :]
    qb = _lane_bcast_col(q_row, PAGE_SIZE)
    lane = lax.broadcasted_iota(jnp.int32, (N_HEADS, LANES), 1)
    sub = lax.broadcasted_iota(jnp.int32, (N_HEADS, LANES), 0)

    def outer(g, gate):
        blk0 = g * SELECT_GROUP
        slots = [(blk0 + i) & (SELECT_RING - 1) for i in range(SELECT_GROUP)]
        for i in range(SELECT_GROUP):
            for cp in copies(n, blk0 + i, slots[i]):
                cp.wait()
        for i in range(SELECT_GROUP):
            t = (buf_ref[slots[i], 0] + buf_ref[slots[i], 1]) * qb
            s_blk = jnp.sum(t.reshape(N_HEADS, HEAD_DIM, PAGE_SIZE), axis=1)
            gsum = jnp.sum(s_blk, axis=1, keepdims=True)
            gate = jnp.where(lane == blk0 + i, gsum, gate)
        nxt = blk0 + SELECT_RING
        wrap = nxt >= n_past_blk
        nsample = jnp.where(wrap, n + 1, n)
        nblk = jnp.where(wrap, nxt - n_past_blk, nxt)

        @pl.when(nsample < n_samples)
        def _():
            for i in range(SELECT_GROUP):
                for cp in copies(nsample, nblk + i, slots[i]):
                    cp.start()
        return gate

    gate = lax.fori_loop(0, n_past_blk // SELECT_GROUP, outer, jnp.zeros((N_HEADS, LANES), F32))
    own = _head_sums(q_row * k_ref[pl.ds(n, 1), :])
    for h in range(N_HEADS):
        gate = jnp.where((lane == n_past_blk) & (sub == h), own[h], gate)
    gate = gate * (1.0 / MOBA_BLOCK)
    gm = jnp.where(lane < n_past_blk, gate, -jnp.inf)
    lane_f = lane.astype(F32)
    out = jnp.zeros((N_HEADS, LANES), F32)
    for r in range(MOBA_TOPK):
        mx = jnp.max(gm, axis=1, keepdims=True)
        pick = jnp.min(jnp.where(gm == mx, lane_f, float(LANES)), axis=1, keepdims=True)
        out = jnp.where(lane == r, pick, out)
        gm = jnp.where(lane_f == pick, -jnp.inf, gm)
    idx_ref[0] = out.astype(jnp.int32)


def _decode_attn_kernel(pt_ref, idx_ref, q_ref, k_ref, v_ref, ck_ref, cv_ref, o_ref,
                        kbuf_ref, vbuf_ref, ksem_ref, vsem_ref, *, n_samples, n_pages, past_len):
    n = pl.program_id(0)
    n_tiles = MOBA_TOPK * PAGES_PER_BLOCK
    per_sample = N_HEADS * n_tiles

    def copies(sample, slot):
        out = []
        for h in range(N_HEADS):
            for j in range(MOBA_TOPK):
                blk = idx_ref[sample * (N_HEADS * MOBA_TOPK) + h * MOBA_TOPK + j]
                for pg in range(PAGES_PER_BLOCK):
                    phys = pt_ref[sample * n_pages + blk * PAGES_PER_BLOCK + pg]
                    i = (h * MOBA_TOPK + j) * PAGES_PER_BLOCK + pg
                    out.append(pltpu.make_async_copy(ck_ref.at[phys, h], kbuf_ref.at[slot, i], ksem_ref.at[slot]))
                    out.append(pltpu.make_async_copy(cv_ref.at[phys, h], vbuf_ref.at[slot, i], vsem_ref.at[slot]))
        return out

    @pl.when(n == 0)
    def _():
        for cp in copies(0, 0):
            cp.start()

    @pl.when(n + 1 < n_samples)
    def _():
        for cp in copies(n + 1, (n + 1) % 2):
            cp.start()

    slot = n % 2
    for i in range(per_sample):
        pltpu.make_async_copy(ck_ref.at[0, 0], kbuf_ref.at[slot, i], ksem_ref.at[slot]).wait()
        pltpu.make_async_copy(cv_ref.at[0, 0], vbuf_ref.at[slot, i], vsem_ref.at[slot]).wait()

    q_row = q_ref[pl.ds(n, 1), :]
    k_row = k_ref[pl.ds(n, 1), :]
    v_row = v_ref[pl.ds(n, 1), :]
    qb = _lane_bcast_col(q_row, PAGE_SIZE)
    s_new = _head_sums(q_row * k_row)
    lane_row = lax.broadcasted_iota(jnp.int32, (1, PAGE_SIZE), 1)
    lane_w = lax.broadcasted_iota(jnp.int32, (1, ATTN_WIDTH), 1)
    acc_heads = []
    p_new_row = jnp.zeros((1, ATTN_WIDTH), F32)
    for h, slope in enumerate(_slopes()):
        qh = qb[h * HEAD_DIM:(h + 1) * HEAD_DIM, :]
        scores = []
        for j in range(MOBA_TOPK):
            blk = idx_ref[n * (N_HEADS * MOBA_TOPK) + h * MOBA_TOPK + j]
            for pg in range(PAGES_PER_BLOCK):
                i = (h * MOBA_TOPK + j) * PAGES_PER_BLOCK + pg
                s = jnp.sum(kbuf_ref[slot, i] * qh, axis=0, keepdims=True) * SCALE
                key_pos = blk * MOBA_BLOCK + pg * PAGE_SIZE + lane_row
                scores.append(s - slope * (past_len - key_pos).astype(F32))
        s_n = s_new[h] * SCALE
        m = s_n
        for s in scores:
            m = jnp.maximum(m, jnp.max(s, axis=1, keepdims=True))
        p_n = jnp.exp(s_n - m)
        l = p_n
        acc = jnp.zeros((HEAD_DIM, PAGE_SIZE), F32)
        for t, s in enumerate(scores):
            p = jnp.exp(s - m)
            l = l + jnp.sum(p, axis=1, keepdims=True)
            acc = acc + vbuf_ref[slot, h * n_tiles + t] * p
        acc_heads.append(acc / l)
        p_new_row = jnp.where(lane_w // HEAD_DIM == h, p_n / l, p_new_row)
    acc_all = jnp.concatenate(acc_heads, axis=0)
    o_row = jnp.sum(acc_all.T, axis=0, keepdims=True) + p_new_row * v_row
    o_ref[0] = o_row


def kernel(x_prompt, x_sample, cache_k, cache_v, page_table, g_pre_mix, w_in, g_sgu, w_spatial, b_spatial,
           w_a, w_b, w_o, g_post_mix, g_pre_ffn, w_gate, w_up, w_down, g_post_ffn):
    depth = w_in.shape[0]
    assert depth == 1, "kernels are written for a single layer"
    bsz, seq, _ = x_prompt.shape
    n_samples, dec_seq, _ = x_sample.shape
    assert dec_seq == 1
    n_pool = cache_k.shape[1]
    n_pages = page_table.shape[1]
    past_len = n_pages * PAGE_SIZE
    n_past_blk = past_len // MOBA_BLOCK
    n_blk = seq // MOBA_BLOCK
    assert n_blk * N_HEADS <= HEAD_DIM and n_past_blk < LANES
    assert n_past_blk % SELECT_RING == 0 and SELECT_RING % SELECT_GROUP == 0
    assert SELECT_RING & (SELECT_RING - 1) == 0

    l = 0
    w_in_b = w_in[l].astype(BF16)
    back_w = (w_a[l].astype(BF16), w_b[l].astype(BF16), w_o[l].astype(BF16), g_post_mix[l][None], g_pre_ffn[l][None],
              w_gate[l].astype(BF16), w_up[l].astype(BF16), w_down[l].astype(BF16), g_post_ffn[l][None])
    gpre = g_pre_mix[l][None]
    gsgu = g_sgu[l][None]
    in_cols = w_in_b.shape[1]

    n_tiles = seq // TILE_FRONT
    tile3 = lambda w: pl.BlockSpec((1, TILE_FRONT, w), lambda b, j: (b, j, 0))
    tile3t = lambda w: pl.BlockSpec((1, w, TILE_FRONT), lambda b, j: (b, 0, j))
    a_p, qext, kt, vt, vb, ga, gb = pl.pallas_call(
        _front_kernel,
        grid=(bsz, n_tiles),
        in_specs=[tile3(D_MODEL), _resident((1, D_MODEL)), _resident((D_MODEL, in_cols)), _resident((1, SGU_WIDTH)),
                  _resident((SGU_GROUPS, CHUNK, CHUNK)), _resident((CHUNK, SGU_GROUPS))],
        out_specs=[tile3(SGU_WIDTH),
                   pl.BlockSpec((1, N_HEADS, TILE_FRONT, PAIR), lambda b, j: (b, 0, j, 0)),
                   tile3t(ATTN_WIDTH), tile3t(ATTN_WIDTH), tile3(ATTN_WIDTH), tile3(D_MODEL), tile3(D_MODEL)],
        out_shape=[jax.ShapeDtypeStruct((bsz, seq, SGU_WIDTH), BF16),
                   jax.ShapeDtypeStruct((bsz, N_HEADS, seq, PAIR), BF16),
                   jax.ShapeDtypeStruct((bsz, ATTN_WIDTH, seq), F32),
                   jax.ShapeDtypeStruct((bsz, ATTN_WIDTH, seq), F32),
                   jax.ShapeDtypeStruct((bsz, seq, ATTN_WIDTH), BF16),
                   jax.ShapeDtypeStruct((bsz, seq, D_MODEL), F32),
                   jax.ShapeDtypeStruct((bsz, seq, D_MODEL), F32)],
        scratch_shapes=[pltpu.VMEM((n_blk, ATTN_WIDTH), F32)],
        compiler_params=pltpu.CompilerParams(dimension_semantics=("arbitrary", "arbitrary"),
                                             vmem_limit_bytes=VMEM_LIMIT_FRONT),
        name="prompt_front",
    )(x_prompt, gpre, w_in_b, gsgu, w_spatial[l], b_spatial[l].T)

    kext = _key_ext_const(seq)
    att_p = pl.pallas_call(
        _attn_kernel,
        grid=(bsz, N_PAIRS),
        in_specs=[pl.BlockSpec((1, 2, seq, PAIR), lambda b, p: (b, p, 0, 0)),
                  pl.BlockSpec((1, PAIR, seq), lambda b, p: (b, p, 0)),
                  pl.BlockSpec((1, seq, PAIR), lambda b, p: (b, 0, p)),
                  pl.BlockSpec((2, PAIR, seq), lambda b, p: (p, 0, 0))],
        out_specs=pl.BlockSpec((1, seq, PAIR), lambda b, p: (b, 0, p)),
        out_shape=jax.ShapeDtypeStruct((bsz, seq, ATTN_WIDTH), BF16),
        scratch_shapes=[pltpu.VMEM((2, PAIR, seq), BF16)],
        compiler_params=pltpu.CompilerParams(dimension_semantics=("arbitrary", "arbitrary"),
                                             vmem_limit_bytes=VMEM_LIMIT_ATTN),
        name="prompt_moba",
    )(qext, kt, vb, kext)

    n_tok = bsz * seq
    flat = lambda z: z.reshape(n_tok, z.shape[-1])
    y_prompt = _back(flat(x_prompt), flat(a_p), flat(att_p), flat(ga), flat(gb), *back_w).reshape(bsz, seq, D_MODEL)

    xs = x_sample.reshape(n_samples, D_MODEL)
    coef = jnp.repeat(w_spatial[l, :, 0, 0], CHUNK)[None]
    bias = jnp.repeat(b_spatial[l, :, 0], CHUNK)[None]
    full = lambda shape: pl.BlockSpec(shape, lambda *_: (0,) * len(shape))
    s_w = lambda w, dt: jax.ShapeDtypeStruct((n_samples, w), dt)
    a_s, vn_s, q_s, k_s, v_s, ga_s, gb_s = pl.pallas_call(
        _front_sample_kernel,
        grid=(1,),
        in_specs=[full((n_samples, D_MODEL)), full((1, D_MODEL)), full((D_MODEL, in_cols)), full((1, SGU_WIDTH)),
                  full((1, SGU_WIDTH)), full((1, SGU_WIDTH))],
        out_specs=[full((n_samples, SGU_WIDTH)), full((n_samples, SGU_WIDTH)), full((n_samples, ATTN_WIDTH)),
                   full((n_samples, ATTN_WIDTH)), full((n_samples, ATTN_WIDTH)),
                   full((n_samples, D_MODEL)), full((n_samples, D_MODEL))],
        out_shape=[s_w(SGU_WIDTH, BF16), s_w(SGU_WIDTH, F32), s_w(ATTN_WIDTH, F32), s_w(ATTN_WIDTH, F32),
                   s_w(ATTN_WIDTH, F32), s_w(D_MODEL, F32), s_w(D_MODEL, F32)],
        compiler_params=pltpu.CompilerParams(dimension_semantics=("arbitrary",),
                                             vmem_limit_bytes=VMEM_LIMIT_FRONT),
        name="sample_front",
    )(xs, gpre, w_in_b, gsgu, coef, bias)

    ck_t = jnp.transpose(cache_k[l], (0, 2, 3, 1))
    cv_t = jnp.transpose(cache_v[l], (0, 2, 3, 1))
    pt_flat = page_table.reshape(-1)

    idx = pl.pallas_call(
        functools.partial(_select_kernel, n_samples=n_samples, n_pages=n_pages, n_past_blk=n_past_blk),
        grid_spec=pltpu.PrefetchScalarGridSpec(
            num_scalar_prefetch=1,
            grid=(n_samples,),
            in_specs=[pl.BlockSpec((n_samples, ATTN_WIDTH), lambda i, pt: (0, 0)),
                      pl.BlockSpec((n_samples, ATTN_WIDTH), lambda i, pt: (0, 0)),
                      pl.BlockSpec(memory_space=pl.ANY)],
            out_specs=pl.BlockSpec((1, N_HEADS, LANES), lambda i, pt: (i, 0, 0)),
            scratch_shapes=[pltpu.VMEM((SELECT_RING, PAGES_PER_BLOCK, ATTN_WIDTH, PAGE_SIZE), F32),
                            pltpu.SemaphoreType.DMA((SELECT_RING,))]),
        out_shape=jax.ShapeDtypeStruct((n_samples, N_HEADS, LANES), jnp.int32),
        compiler_params=pltpu.CompilerParams(dimension_semantics=("arbitrary",),
                                             vmem_limit_bytes=VMEM_LIMIT_DECODE),
        name="sample_block_select",
    )(pt_flat, q_s, k_s, ck_t.reshape(n_pool, ATTN_WIDTH, PAGE_SIZE))
    idx_flat = idx[:, :, :MOBA_TOPK].reshape(-1)

    n_gather = N_HEADS * MOBA_TOPK * PAGES_PER_BLOCK
    att_s = pl.pallas_call(
        functools.partial(_decode_attn_kernel, n_samples=n_samples, n_pages=n_pages, past_len=past_len),
        grid_spec=pltpu.PrefetchScalarGridSpec(
            num_scalar_prefetch=2,
            grid=(n_samples,),
            in_specs=[pl.BlockSpec((n_samples, ATTN_WIDTH), lambda i, pt, ix: (0, 0)),
                      pl.BlockSpec((n_samples, ATTN_WIDTH), lambda i, pt, ix: (0, 0)),
                      pl.BlockSpec((n_samples, ATTN_WIDTH), lambda i, pt, ix: (0, 0)),
                      pl.BlockSpec(memory_space=pl.ANY), pl.BlockSpec(memory_space=pl.ANY)],
            out_specs=pl.BlockSpec((1, 1, ATTN_WIDTH), lambda i, pt, ix: (i, 0, 0)),
            scratch_shapes=[pltpu.VMEM((2, n_gather, HEAD_DIM, PAGE_SIZE), F32),
                            pltpu.VMEM((2, n_gather, HEAD_DIM, PAGE_SIZE), F32),
                            pltpu.SemaphoreType.DMA((2,)), pltpu.SemaphoreType.DMA((2,))]),
        out_shape=jax.ShapeDtypeStruct((n_samples, 1, ATTN_WIDTH), F32),
        compiler_params=pltpu.CompilerParams(dimension_semantics=("arbitrary",),
                                             vmem_limit_bytes=VMEM_LIMIT_DECODE),
        name="sample_moba",
    )(pt_flat, idx_flat, q_s, k_s, v_s, ck_t, cv_t)
    att_s = att_s.reshape(n_samples, ATTN_WIDTH).astype(BF16)

    y_sample = _back(xs, a_s, att_s, ga_s, gb_s, *back_w).reshape(n_samples, 1, D_MODEL)

    heads_t = lambda zt: jnp.transpose(zt.reshape(1, bsz, N_HEADS, HEAD_DIM, seq), (0, 1, 4, 2, 3))
    heads_s = lambda z: z.reshape(1, n_samples, 1, N_HEADS, HEAD_DIM)
    return (y_prompt, y_sample, heads_t(kt), heads_t(vt), heads_s(k_s), heads_s(v_s),
            vn_s.reshape(1, n_samples, 1, SGU_WIDTH))
```

```python
import functools

import numpy as np
import jax
import jax.numpy as jnp
from jax import lax
from jax.experimental import pallas as pl
from jax.experimental.pallas import tpu as pltpu

F32 = jnp.float32
BF16 = jnp.bfloat16

D_MODEL = 1024
N_HEADS = 8
HEAD_DIM = 64
ATTN_WIDTH = N_HEADS * HEAD_DIM
MOBA_BLOCK = 256
MOBA_TOPK = 3
CHUNK = 128
SGU_GROUPS = 4
SGU_WIDTH = 512
D_FF = 2816
PAGE_SIZE = 128
EPS = 1e-6
ALIBI_MAX_BIAS = 8.0

LANES = 128
PAIR = 2 * HEAD_DIM
N_PAIRS = N_HEADS // 2
NEG_BIG = -1e30
PAGES_PER_BLOCK = MOBA_BLOCK // PAGE_SIZE
SCALE = HEAD_DIM ** -0.5

TILE_FRONT = MOBA_BLOCK
TILE_BACK = 512
FF_CHUNK = 256
SELECT_RING = 16
SELECT_GROUP = 4

VMEM_LIMIT_FRONT = 44 * 1024 * 1024
VMEM_LIMIT_ATTN = 48 * 1024 * 1024
VMEM_LIMIT_BACK = 58 * 1024 * 1024
VMEM_LIMIT_DECODE = 32 * 1024 * 1024


def _slopes():
    return [2.0 ** (-(ALIBI_MAX_BIAS / N_HEADS) * (h + 1)) for h in range(N_HEADS)]


def _rms(x, g):
    return x * lax.rsqrt(jnp.mean(x * x, axis=-1, keepdims=True) + EPS) * g


def _split_bf16(x):
    hi = x.astype(BF16)
    lo = (x - hi.astype(F32)).astype(BF16)
    return hi, lo


def _dot_nt_f32(a, b):
    dn = (((1,), (1,)), ((), ()))
    ah, al = _split_bf16(a)
    bh, bl = _split_bf16(b)
    d = functools.partial(lax.dot_general, dimension_numbers=dn, preferred_element_type=F32)
    return d(ah, bh) + d(ah, bl) + d(al, bh)


def _resident(shape):
    nd = len(shape)
    return pl.BlockSpec(shape, lambda *_: (0,) * nd, pipeline_mode=pl.Buffered(1))


def _front_kernel(x_ref, gpre_ref, win_ref, gsgu_ref, wsp_ref, bspt_ref,
                  a_ref, qext_ref, kt_ref, vt_ref, vb_ref, ga_ref, gb_ref, ksum_ref):
    j = pl.program_id(1)

    @pl.when(j == 0)
    def _():
        ksum_ref[...] = jnp.zeros_like(ksum_ref)

    h = _rms(x_ref[0], gpre_ref[...]).astype(BF16)

    def proj(lo, hi):
        return jnp.dot(h, win_ref[:, lo:hi], preferred_element_type=F32)

    c0 = SGU_WIDTH
    c1 = 2 * SGU_WIDTH
    c2 = c1 + ATTN_WIDTH
    c3 = c2 + ATTN_WIDTH
    c4 = c3 + ATTN_WIDTH
    c5 = c4 + D_MODEL
    u = proj(0, c0)
    v = proj(c0, c1)
    q = proj(c1, c2)
    k = proj(c2, c3)
    vv = proj(c3, c4)
    ga_ref[0] = proj(c4, c5)
    gb_ref[0] = proj(c5, c5 + D_MODEL)

    vn = _rms(v, gsgu_ref[...]).astype(BF16)
    tri_r = lax.broadcasted_iota(jnp.int32, (CHUNK, CHUNK), 0)
    tri_c = lax.broadcasted_iota(jnp.int32, (CHUNK, CHUNK), 1)
    for g in range(SGU_GROUPS):
        wg = jnp.where(tri_c <= tri_r, wsp_ref[g], 0.0).astype(BF16)
        cols = slice(g * CHUNK, (g + 1) * CHUNK)
        for c in range(TILE_FRONT // CHUNK):
            rows = slice(c * CHUNK, (c + 1) * CHUNK)
            sv = jnp.dot(wg, vn[rows, cols], preferred_element_type=F32) + bspt_ref[:, g:g + 1]
            a_ref[0, rows, cols] = (u[rows, cols] * sv).astype(BF16)

    kt_ref[0] = k.T
    vt_ref[0] = vv.T
    vb_ref[0] = vv.astype(BF16)

    ksum_ref[pl.ds(j, 1), :] = jnp.sum(k, axis=0, keepdims=True)
    km = ksum_ref[...] * (1.0 / MOBA_BLOCK)
    n_blk = km.shape[0]
    km_rep = jnp.concatenate([km] * N_HEADS + [jnp.zeros((LANES - N_HEADS * n_blk, ATTN_WIDTH), F32)], axis=0)
    r_i = lax.broadcasted_iota(jnp.int32, (LANES, ATTN_WIDTH), 0)
    c_i = lax.broadcasted_iota(jnp.int32, (LANES, ATTN_WIDTH), 1)
    head_rows = (r_i // n_blk == c_i // HEAD_DIM) & (r_i < N_HEADS * n_blk)
    km_bd = jnp.where(head_rows, km_rep, 0.0)
    gate_t = _dot_nt_f32(km_bd, q)

    blk_i = lax.broadcasted_iota(jnp.int32, (n_blk, TILE_FRONT), 0)
    cand = blk_i < j
    bias_rows = []
    for hh in range(N_HEADS):
        gm = jnp.where(cand, gate_t[hh * n_blk:(hh + 1) * n_blk, :], -jnp.inf)
        rank = jnp.zeros((n_blk, TILE_FRONT), F32)
        for m in range(n_blk):
            gmm = gm[m:m + 1, :]
            ahead = (gmm > gm) | ((gmm == gm) & (blk_i > m))
            rank = rank + jnp.where(ahead, 1.0, 0.0)
        keep = (cand & (rank < float(MOBA_TOPK))) | (blk_i == j)
        bias_rows.append(jnp.where(keep, 0.0, NEG_BIG))
    bias_rows.append(jnp.full((LANES - N_HEADS * n_blk, TILE_FRONT), NEG_BIG, F32))
    sel = jnp.concatenate(bias_rows, axis=0).T
    sel_sw = pltpu.roll(sel, LANES // 2, axis=1)

    lane = lax.broadcasted_iota(jnp.int32, (TILE_FRONT, LANES), 1)
    for hh in range(N_HEADS):
        p = hh // 2
        qp = q[:, p * PAIR:(p + 1) * PAIR] * SCALE
        if hh % 2 == 0:
            lo = HEAD_DIM + hh * n_blk
            ext = jnp.where((lane >= lo) & (lane < lo + n_blk), sel_sw,
                            jnp.where(lane >= LANES - 2, 1.0, 0.0))
            qx = jnp.where(lane < HEAD_DIM, qp, ext)
        else:
            lo = hh * n_blk
            ext = jnp.where((lane >= lo) & (lane < lo + n_blk), sel,
                            jnp.where(lane < 2, 1.0, 0.0))
            qx = jnp.where(lane >= HEAD_DIM, qp, ext)
        qext_ref[0, hh] = qx.astype(BF16)


def _key_ext_const(seq):
    n_blk = seq // MOBA_BLOCK
    pos = np.arange(seq)
    off = (pos % MOBA_BLOCK).astype(np.float32)
    blk = pos // MOBA_BLOCK
    out = np.zeros((N_HEADS, PAIR, seq), np.float32)
    for hh, slope in enumerate(_slopes()):
        if hh % 2 == 0:
            base, r_off, r_blk = HEAD_DIM + hh * n_blk, PAIR - 2, PAIR - 1
        else:
            base, r_off, r_blk = hh * n_blk, 0, 1
        for n in range(n_blk):
            out[hh, base + n] = (blk == n)
        out[hh, r_off] = slope * off
        out[hh, r_blk] = slope * MOBA_BLOCK * blk
    return jnp.asarray(out, dtype=BF16)


def _attn_kernel(pt_ref, qext_ref, kt_ref, vb_ref, kext_ref, qs_ref, ks_ref, ck_ref, o_ref, idx_ref,
                 kx_ref, buf_ref, sem_ref, qb_ref, *, scorer_cfg):
    seq = kt_ref.shape[2]
    n_blk = seq // MOBA_BLOCK
    step = pl.program_id(0) * pl.num_programs(1) + pl.program_id(1)
    scorer = _PagedScorer(pt_ref, qs_ref, ks_ref, ck_ref, idx_ref, buf_ref, sem_ref, qb_ref,
                          step=step, **scorer_cfg)
    assert scorer.n_groups <= n_blk
    scorer.begin()

    kt = kt_ref[0].astype(BF16)
    row = lax.broadcasted_iota(jnp.int32, (PAIR, seq), 0)
    kx_ref[0] = jnp.where(row < HEAD_DIM, kt, kext_ref[0])
    kx_ref[1] = jnp.where(row >= HEAD_DIM, kt, kext_ref[1])

    r_i = lax.broadcasted_iota(jnp.int32, (MOBA_BLOCK, MOBA_BLOCK), 0)
    c_i = lax.broadcasted_iota(jnp.int32, (MOBA_BLOCK, MOBA_BLOCK), 1)
    causal = c_i <= r_i
    lane = lax.broadcasted_iota(jnp.int32, (MOBA_BLOCK, PAIR), 1)
    for j in range(n_blk):
        riding = j < scorer.n_groups
        if riding:
            scorer.wait(j)
            zeros = scorer.score(j)
        w = (j + 1) * MOBA_BLOCK
        rows = slice(j * MOBA_BLOCK, w)
        outs = []
        for e in range(2):
            s = jnp.dot(qext_ref[0, e, rows, :], kx_ref[e, :, :w], preferred_element_type=F32)
            own = jnp.where(causal, s[:, w - MOBA_BLOCK:], NEG_BIG)
            s = own if j == 0 else jnp.concatenate([s[:, :w - MOBA_BLOCK], own], axis=1)
            m = jnp.max(s, axis=-1, keepdims=True)
            p = jnp.exp(s - m)
            l = jnp.sum(p, axis=-1, keepdims=True)
            if riding and e == 1:
                p = _add_zero_rows(p, zeros)
            o = jnp.dot(p.astype(BF16), vb_ref[0, :w, :], preferred_element_type=F32)
            outs.append(o / l)
        o_ref[0, rows, :] = jnp.where(lane < HEAD_DIM, outs[0], outs[1]).astype(BF16)
        if riding:
            scorer.refill(j)
    scorer.finish()


def _back_kernel(*refs, scorer_cfg):
    if scorer_cfg is None:
        scorer = None
        (x_ref, a_ref, att_ref, ga_ref, gb_ref, wa_ref, wb_ref, wo_ref, gpm_ref, gpf_ref,
         wg_ref, wu_ref, wd_ref, gpo_ref, o_ref) = refs
    else:
        (pt_ref, x_ref, a_ref, att_ref, ga_ref, gb_ref, wa_ref, wb_ref, wo_ref, gpm_ref, gpf_ref,
         wg_ref, wu_ref, wd_ref, gpo_ref, qs_ref, ks_ref, ck_ref, o_ref, idx_ref,
         buf_ref, sem_ref, qb_ref) = refs
        scorer = _PagedScorer(pt_ref, qs_ref, ks_ref, ck_ref, idx_ref, buf_ref, sem_ref, qb_ref,
                              step=pl.program_id(0), **scorer_cfg)
        assert scorer.n_groups <= D_FF // FF_CHUNK
        scorer.begin()
    ya = jnp.dot(a_ref[...], wa_ref[...], preferred_element_type=F32)
    yb = jnp.dot(att_ref[...], wb_ref[...], preferred_element_type=F32)
    m = jax.nn.sigmoid(ga_ref[...]) * ya + jax.nn.sigmoid(gb_ref[...]) * yb
    y = jnp.dot(m.astype(BF16), wo_ref[...], preferred_element_type=F32)
    x1 = x_ref[...] + _rms(y, gpm_ref[...])
    h2 = _rms(x1, gpf_ref[...]).astype(BF16)
    acc = jnp.zeros(x1.shape, F32)
    for c in range(D_FF // FF_CHUNK):
        riding = scorer is not None and c < scorer.n_groups
        if riding:
            scorer.wait(c)
        cols = slice(c * FF_CHUNK, (c + 1) * FF_CHUNK)
        g = jnp.dot(h2, wg_ref[:, cols], preferred_element_type=F32)
        up = jnp.dot(h2, wu_ref[:, cols], preferred_element_type=F32)
        act = (g * jax.nn.sigmoid(g)) * up
        if riding:
            act = _add_zero_rows(act, scorer.score(c))
        acc = acc + jnp.dot(act.astype(BF16), wd_ref[cols, :], preferred_element_type=F32)
        if riding:
            scorer.refill(c)
    o_ref[...] = x1 + _rms(acc, gpo_ref[...])
    if scorer is not None:
        scorer.finish()


def _add_zero_rows(x, zeros):
    r, c = zeros.shape
    z = jnp.concatenate([zeros] * (x.shape[1] // c), axis=1)
    return jnp.concatenate([x[:r] + z, x[r:]], axis=0)


def _scorer_scratch():
    return [pltpu.VMEM((SELECT_RING, PAGES_PER_BLOCK, ATTN_WIDTH, PAGE_SIZE), F32),
            pltpu.SemaphoreType.DMA((SELECT_RING,)),
            pltpu.VMEM((ATTN_WIDTH, PAGE_SIZE), F32)]


def _back(x, a, att, ga, gb, w_a, w_b, w_o, gpm, gpf, w_g, w_u, w_d, gpo, paged=None):
    n = x.shape[0]
    tm = min(TILE_BACK, n)
    steps = n // tm
    row = lambda w: pl.BlockSpec((tm, w), lambda i, *_: (i, 0))
    in_specs = [row(D_MODEL), row(SGU_WIDTH), row(ATTN_WIDTH), row(D_MODEL), row(D_MODEL),
                _resident(w_a.shape), _resident(w_b.shape), _resident(w_o.shape),
                _resident(gpm.shape), _resident(gpf.shape),
                _resident(w_g.shape), _resident(w_u.shape), _resident(w_d.shape), _resident(gpo.shape)]
    args = (x, a, att, ga, gb, w_a, w_b, w_o, gpm, gpf, w_g, w_u, w_d, gpo)
    params = pltpu.CompilerParams(dimension_semantics=("arbitrary",), vmem_limit_bytes=VMEM_LIMIT_BACK)
    if paged is None:
        return pl.pallas_call(
            functools.partial(_back_kernel, scorer_cfg=None),
            grid=(steps,), in_specs=in_specs, out_specs=row(D_MODEL),
            out_shape=jax.ShapeDtypeStruct((n, D_MODEL), F32),
            compiler_params=params, name="mixer_back_ffn",
        )(*args)
    pt_flat, q_s, k_s, ck_pages, cfg = paged
    assert cfg["n_local"] == steps
    return pl.pallas_call(
        functools.partial(_back_kernel, scorer_cfg=cfg),
        grid_spec=pltpu.PrefetchScalarGridSpec(
            num_scalar_prefetch=1, grid=(steps,),
            in_specs=in_specs + [_resident(q_s.shape), _resident(k_s.shape), pl.BlockSpec(memory_space=pl.ANY)],
            out_specs=[row(D_MODEL), pl.BlockSpec((1, N_HEADS, LANES), lambda i, *_: (i, 0, 0))],
            scratch_shapes=_scorer_scratch()),
        out_shape=[jax.ShapeDtypeStruct((n, D_MODEL), F32),
                   jax.ShapeDtypeStruct((steps, N_HEADS, LANES), jnp.int32)],
        compiler_params=params, name="mixer_back_ffn_scoring",
    )(pt_flat, *args, q_s, k_s, ck_pages)


def _front_sample_kernel(x_ref, gpre_ref, win_ref, gsgu_ref, coef_ref, bias_ref,
                         a_ref, vn_ref, q_ref, k_ref, v_ref, ga_ref, gb_ref):
    h = _rms(x_ref[...], gpre_ref[...]).astype(BF16)

    def proj(lo, hi):
        return jnp.dot(h, win_ref[:, lo:hi], preferred_element_type=F32)

    c0 = SGU_WIDTH
    c1 = 2 * SGU_WIDTH
    c2 = c1 + ATTN_WIDTH
    c3 = c2 + ATTN_WIDTH
    c4 = c3 + ATTN_WIDTH
    c5 = c4 + D_MODEL
    u = proj(0, c0)
    vn = _rms(proj(c0, c1), gsgu_ref[...])
    vn_ref[...] = vn
    a_ref[...] = (u * (vn * coef_ref[...] + bias_ref[...])).astype(BF16)
    q_ref[...] = proj(c1, c2)
    k_ref[...] = proj(c2, c3)
    v_ref[...] = proj(c3, c4)
    ga_ref[...] = proj(c4, c5)
    gb_ref[...] = proj(c5, c5 + D_MODEL)


def _lane_bcast_col(row, n_rows):
    return jnp.broadcast_to(row, (n_rows, row.shape[1])).T


def _head_sums(row):
    return [jnp.sum(row[:, h * HEAD_DIM:(h + 1) * HEAD_DIM], axis=1, keepdims=True) for h in range(N_HEADS)]


class _PagedScorer:
    def __init__(self, pt_ref, q_ref, k_ref, ck_ref, idx_ref, buf_ref, sem_ref, qb_ref, *,
                 step, first_sample, n_local, n_pages, n_past_blk):
        self.pt_ref, self.q_ref, self.k_ref, self.ck_ref, self.idx_ref = pt_ref, q_ref, k_ref, ck_ref, idx_ref
        self.buf_ref, self.sem_ref, self.qb_ref = buf_ref, sem_ref, qb_ref
        self.step, self.n_local, self.n_pages, self.n_past_blk = step, n_local, n_pages, n_past_blk
        self.sample = first_sample + step
        self.first_sample = first_sample
        self.n_groups = n_past_blk // SELECT_GROUP

    def _copies(self, sample, blk):
        slot = blk % SELECT_RING
        base = sample * self.n_pages + blk * PAGES_PER_BLOCK
        return [pltpu.make_async_copy(self.ck_ref.at[self.pt_ref[base + pg]], self.buf_ref.at[slot, pg],
                                      self.sem_ref.at[slot]) for pg in range(PAGES_PER_BLOCK)]

    def begin(self):
        @pl.when(self.step == 0)
        def _():
            for blk in range(SELECT_RING):
                for cp in self._copies(self.first_sample, blk):
                    cp.start()

        q_row = self.q_ref[pl.ds(self.sample, 1), :]
        self.qb_ref[...] = _lane_bcast_col(q_row, PAGE_SIZE)
        self.gate = jnp.zeros((N_HEADS, LANES), F32)

    def wait(self, g):
        for blk in range(g * SELECT_GROUP, (g + 1) * SELECT_GROUP):
            for cp in self._copies(self.sample, blk):
                cp.wait()

    def score(self, g):
        lane = lax.broadcasted_iota(jnp.int32, (N_HEADS, LANES), 1)
        qb = self.qb_ref[...]
        for blk in range(g * SELECT_GROUP, (g + 1) * SELECT_GROUP):
            slot = blk % SELECT_RING
            t = (self.buf_ref[slot, 0] + self.buf_ref[slot, 1]) * qb
            s_blk = jnp.sum(t.reshape(N_HEADS, HEAD_DIM, PAGE_SIZE), axis=1)
            gsum = jnp.sum(s_blk, axis=1, keepdims=True)
            self.gate = jnp.where(lane == blk, gsum, self.gate)
        bits = lax.bitcast_convert_type(self.gate, jnp.uint32)
        sixteen = jnp.uint32(16)
        return lax.shift_right_logical(lax.shift_right_logical(bits, sixteen), sixteen).astype(F32)

    def refill(self, g):
        nxt = g * SELECT_GROUP + SELECT_RING
        if nxt < self.n_past_blk:
            for i in range(SELECT_GROUP):
                for cp in self._copies(self.sample, nxt + i):
                    cp.start()
        else:
            @pl.when(self.step + 1 < self.n_local)
            def _():
                for i in range(SELECT_GROUP):
                    for cp in self._copies(self.sample + 1, nxt - self.n_past_blk + i):
                        cp.start()

    def finish(self):
        lane = lax.broadcasted_iota(jnp.int32, (N_HEADS, LANES), 1)
        sub = lax.broadcasted_iota(jnp.int32, (N_HEADS, LANES), 0)
        gate = self.gate
        q_row = self.q_ref[pl.ds(self.sample, 1), :]
        own = _head_sums(q_row * self.k_ref[pl.ds(self.sample, 1), :])
        for h in range(N_HEADS):
            gate = jnp.where((lane == self.n_past_blk) & (sub == h), own[h], gate)
        gate = gate * (1.0 / MOBA_BLOCK)
        gm = jnp.where(lane < self.n_past_blk, gate, -jnp.inf)
        lane_f = lane.astype(F32)
        out = jnp.zeros((N_HEADS, LANES), F32)
        for r in range(MOBA_TOPK):
            mx = jnp.max(gm, axis=1, keepdims=True)
            pick = jnp.min(jnp.where(gm == mx, lane_f, float(LANES)), axis=1, keepdims=True)
            out = jnp.where(lane == r, pick, out)
            gm = jnp.where(lane_f == pick, -jnp.inf, gm)
        self.idx_ref[0] = out.astype(jnp.int32)


def _decode_attn_kernel(pt_ref, idx_ref, q_ref, k_ref, v_ref, ck_ref, cv_ref, o_ref,
                        kbuf_ref, vbuf_ref, ksem_ref, vsem_ref, *, n_samples, n_pages, past_len):
    n = pl.program_id(0)
    n_tiles = MOBA_TOPK * PAGES_PER_BLOCK
    per_sample = N_HEADS * n_tiles

    def copies(sample, slot):
        out = []
        for h in range(N_HEADS):
            for j in range(MOBA_TOPK):
                blk = idx_ref[sample * (N_HEADS * MOBA_TOPK) + h * MOBA_TOPK + j]
                for pg in range(PAGES_PER_BLOCK):
                    phys = pt_ref[sample * n_pages + blk * PAGES_PER_BLOCK + pg]
                    i = (h * MOBA_TOPK + j) * PAGES_PER_BLOCK + pg
                    out.append(pltpu.make_async_copy(ck_ref.at[phys, h], kbuf_ref.at[slot, i], ksem_ref.at[slot]))
                    out.append(pltpu.make_async_copy(cv_ref.at[phys, h], vbuf_ref.at[slot, i], vsem_ref.at[slot]))
        return out

    @pl.when(n == 0)
    def _():
        for cp in copies(0, 0):
            cp.start()

    @pl.when(n + 1 < n_samples)
    def _():
        for cp in copies(n + 1, (n + 1) % 2):
            cp.start()

    slot = n % 2
    for i in range(per_sample):
        pltpu.make_async_copy(ck_ref.at[0, 0], kbuf_ref.at[slot, i], ksem_ref.at[slot]).wait()
        pltpu.make_async_copy(cv_ref.at[0, 0], vbuf_ref.at[slot, i], vsem_ref.at[slot]).wait()

    q_row = q_ref[pl.ds(n, 1), :]
    k_row = k_ref[pl.ds(n, 1), :]
    v_row = v_ref[pl.ds(n, 1), :]
    qb = _lane_bcast_col(q_row, PAGE_SIZE)
    s_new = _head_sums(q_row * k_row)
    lane_row = lax.broadcasted_iota(jnp.int32, (1, PAGE_SIZE), 1)
    lane_w = lax.broadcasted_iota(jnp.int32, (1, ATTN_WIDTH), 1)
    acc_heads = []
    p_new_row = jnp.zeros((1, ATTN_WIDTH), F32)
    for h, slope in enumerate(_slopes()):
        qh = qb[h * HEAD_DIM:(h + 1) * HEAD_DIM, :]
        scores = []
        for j in range(MOBA_TOPK):
            blk = idx_ref[n * (N_HEADS * MOBA_TOPK) + h * MOBA_TOPK + j]
            for pg in range(PAGES_PER_BLOCK):
                i = (h * MOBA_TOPK + j) * PAGES_PER_BLOCK + pg
                s = jnp.sum(kbuf_ref[slot, i] * qh, axis=0, keepdims=True) * SCALE
                key_pos = blk * MOBA_BLOCK + pg * PAGE_SIZE + lane_row
                scores.append(s - slope * (past_len - key_pos).astype(F32))
        s_n = s_new[h] * SCALE
        m = s_n
        for s in scores:
            m = jnp.maximum(m, jnp.max(s, axis=1, keepdims=True))
        p_n = jnp.exp(s_n - m)
        l = p_n
        acc = jnp.zeros((HEAD_DIM, PAGE_SIZE), F32)
        for t, s in enumerate(scores):
            p = jnp.exp(s - m)
            l = l + jnp.sum(p, axis=1, keepdims=True)
            acc = acc + vbuf_ref[slot, h * n_tiles + t] * p
        acc_heads.append(acc / l)
        p_new_row = jnp.where(lane_w // HEAD_DIM == h, p_n / l, p_new_row)
    acc_all = jnp.concatenate(acc_heads, axis=0)
    o_row = jnp.sum(acc_all.T, axis=0, keepdims=True) + p_new_row * v_row
    o_ref[0] = o_row


def kernel(x_prompt, x_sample, cache_k, cache_v, page_table, g_pre_mix, w_in, g_sgu, w_spatial, b_spatial,
           w_a, w_b, w_o, g_post_mix, g_pre_ffn, w_gate, w_up, w_down, g_post_ffn):
    depth = w_in.shape[0]
    assert depth == 1, "kernels are written for a single layer"
    bsz, seq, _ = x_prompt.shape
    n_samples, dec_seq, _ = x_sample.shape
    assert dec_seq == 1
    n_pool = cache_k.shape[1]
    n_pages = page_table.shape[1]
    past_len = n_pages * PAGE_SIZE
    n_past_blk = past_len // MOBA_BLOCK
    n_blk = seq // MOBA_BLOCK
    assert n_blk * N_HEADS <= HEAD_DIM and n_past_blk < LANES
    assert n_past_blk % SELECT_RING == 0 and SELECT_RING % SELECT_GROUP == 0

    l = 0
    w_in_b = w_in[l].astype(BF16)
    back_w = (w_a[l].astype(BF16), w_b[l].astype(BF16), w_o[l].astype(BF16), g_post_mix[l][None], g_pre_ffn[l][None],
              w_gate[l].astype(BF16), w_up[l].astype(BF16), w_down[l].astype(BF16), g_post_ffn[l][None])
    gpre = g_pre_mix[l][None]
    gsgu = g_sgu[l][None]
    in_cols = w_in_b.shape[1]

    xs = x_sample.reshape(n_samples, D_MODEL)
    coef = jnp.repeat(w_spatial[l, :, 0, 0], CHUNK)[None]
    bias = jnp.repeat(b_spatial[l, :, 0], CHUNK)[None]
    full = lambda shape: pl.BlockSpec(shape, lambda *_: (0,) * len(shape))
    s_w = lambda w, dt: jax.ShapeDtypeStruct((n_samples, w), dt)
    a_s, vn_s, q_s, k_s, v_s, ga_s, gb_s = pl.pallas_call(
        _front_sample_kernel,
        grid=(1,),
        in_specs=[full((n_samples, D_MODEL)), full((1, D_MODEL)), full((D_MODEL, in_cols)), full((1, SGU_WIDTH)),
                  full((1, SGU_WIDTH)), full((1, SGU_WIDTH))],
        out_specs=[full((n_samples, SGU_WIDTH)), full((n_samples, SGU_WIDTH)), full((n_samples, ATTN_WIDTH)),
                   full((n_samples, ATTN_WIDTH)), full((n_samples, ATTN_WIDTH)),
                   full((n_samples, D_MODEL)), full((n_samples, D_MODEL))],
        out_shape=[s_w(SGU_WIDTH, BF16), s_w(SGU_WIDTH, F32), s_w(ATTN_WIDTH, F32), s_w(ATTN_WIDTH, F32),
                   s_w(ATTN_WIDTH, F32), s_w(D_MODEL, F32), s_w(D_MODEL, F32)],
        compiler_params=pltpu.CompilerParams(dimension_semantics=("arbitrary",),
                                             vmem_limit_bytes=VMEM_LIMIT_FRONT),
        name="sample_front",
    )(xs, gpre, w_in_b, gsgu, coef, bias)

    ck_t = jnp.transpose(cache_k[l], (0, 2, 3, 1))
    cv_t = jnp.transpose(cache_v[l], (0, 2, 3, 1))
    ck_pages = ck_t.reshape(n_pool, ATTN_WIDTH, PAGE_SIZE)
    pt_flat = page_table.reshape(-1)
    n_tok = bsz * seq
    steps_attn = bsz * N_PAIRS
    steps_back = n_tok // TILE_BACK
    assert steps_attn + steps_back == n_samples
    scorer_cfg = lambda first, count: dict(first_sample=first, n_local=count, n_pages=n_pages, n_past_blk=n_past_blk)

    n_tiles = seq // TILE_FRONT
    tile3 = lambda w: pl.BlockSpec((1, TILE_FRONT, w), lambda b, j: (b, j, 0))
    tile3t = lambda w: pl.BlockSpec((1, w, TILE_FRONT), lambda b, j: (b, 0, j))
    a_p, qext, kt, vt, vb, ga, gb = pl.pallas_call(
        _front_kernel,
        grid=(bsz, n_tiles),
        in_specs=[tile3(D_MODEL), _resident((1, D_MODEL)), _resident((D_MODEL, in_cols)), _resident((1, SGU_WIDTH)),
                  _resident((SGU_GROUPS, CHUNK, CHUNK)), _resident((CHUNK, SGU_GROUPS))],
        out_specs=[tile3(SGU_WIDTH),
                   pl.BlockSpec((1, N_HEADS, TILE_FRONT, PAIR), lambda b, j: (b, 0, j, 0)),
                   tile3t(ATTN_WIDTH), tile3t(ATTN_WIDTH), tile3(ATTN_WIDTH), tile3(D_MODEL), tile3(D_MODEL)],
        out_shape=[jax.ShapeDtypeStruct((bsz, seq, SGU_WIDTH), BF16),
                   jax.ShapeDtypeStruct((bsz, N_HEADS, seq, PAIR), BF16),
                   jax.ShapeDtypeStruct((bsz, ATTN_WIDTH, seq), F32),
                   jax.ShapeDtypeStruct((bsz, ATTN_WIDTH, seq), F32),
                   jax.ShapeDtypeStruct((bsz, seq, ATTN_WIDTH), BF16),
                   jax.ShapeDtypeStruct((bsz, seq, D_MODEL), F32),
                   jax.ShapeDtypeStruct((bsz, seq, D_MODEL), F32)],
        scratch_shapes=[pltpu.VMEM((n_blk, ATTN_WIDTH), F32)],
        compiler_params=pltpu.CompilerParams(dimension_semantics=("arbitrary", "arbitrary"),
                                             vmem_limit_bytes=VMEM_LIMIT_FRONT),
        name="prompt_front",
    )(x_prompt, gpre, w_in_b, gsgu, w_spatial[l], b_spatial[l].T)

    kext = _key_ext_const(seq)
    att_p, idx_a = pl.pallas_call(
        functools.partial(_attn_kernel, scorer_cfg=scorer_cfg(0, steps_attn)),
        grid_spec=pltpu.PrefetchScalarGridSpec(
            num_scalar_prefetch=1,
            grid=(bsz, N_PAIRS),
            in_specs=[pl.BlockSpec((1, 2, seq, PAIR), lambda b, p, pt: (b, p, 0, 0)),
                      pl.BlockSpec((1, PAIR, seq), lambda b, p, pt: (b, p, 0)),
                      pl.BlockSpec((1, seq, PAIR), lambda b, p, pt: (b, 0, p)),
                      pl.BlockSpec((2, PAIR, seq), lambda b, p, pt: (p, 0, 0)),
                      _resident(q_s.shape), _resident(k_s.shape), pl.BlockSpec(memory_space=pl.ANY)],
            out_specs=[pl.BlockSpec((1, seq, PAIR), lambda b, p, pt: (b, 0, p)),
                       pl.BlockSpec((1, N_HEADS, LANES), lambda b, p, pt: (b * N_PAIRS + p, 0, 0))],
            scratch_shapes=[pltpu.VMEM((2, PAIR, seq), BF16)] + _scorer_scratch()),
        out_shape=[jax.ShapeDtypeStruct((bsz, seq, ATTN_WIDTH), BF16),
                   jax.ShapeDtypeStruct((steps_attn, N_HEADS, LANES), jnp.int32)],
        compiler_params=pltpu.CompilerParams(dimension_semantics=("arbitrary", "arbitrary"),
                                             vmem_limit_bytes=VMEM_LIMIT_ATTN),
        name="prompt_moba_scoring",
    )(pt_flat, qext, kt, vb, kext, q_s, k_s, ck_pages)

    flat = lambda z: z.reshape(n_tok, z.shape[-1])
    y_prompt, idx_b = _back(flat(x_prompt), flat(a_p), flat(att_p), flat(ga), flat(gb), *back_w,
                            paged=(pt_flat, q_s, k_s, ck_pages, scorer_cfg(steps_attn, steps_back)))
    y_prompt = y_prompt.reshape(bsz, seq, D_MODEL)

    idx_flat = jnp.concatenate([idx_a, idx_b], axis=0)[:, :, :MOBA_TOPK].reshape(-1)
    n_gather = N_HEADS * MOBA_TOPK * PAGES_PER_BLOCK
    att_s = pl.pallas_call(
        functools.partial(_decode_attn_kernel, n_samples=n_samples, n_pages=n_pages, past_len=past_len),
        grid_spec=pltpu.PrefetchScalarGridSpec(
            num_scalar_prefetch=2,
            grid=(n_samples,),
            in_specs=[pl.BlockSpec((n_samples, ATTN_WIDTH), lambda i, pt, ix: (0, 0)),
                      pl.BlockSpec((n_samples, ATTN_WIDTH), lambda i, pt, ix: (0, 0)),
                      pl.BlockSpec((n_samples, ATTN_WIDTH), lambda i, pt, ix: (0, 0)),
                      pl.BlockSpec(memory_space=pl.ANY), pl.BlockSpec(memory_space=pl.ANY)],
            out_specs=pl.BlockSpec((1, 1, ATTN_WIDTH), lambda i, pt, ix: (i, 0, 0)),
            scratch_shapes=[pltpu.VMEM((2, n_gather, HEAD_DIM, PAGE_SIZE), F32),
                            pltpu.VMEM((2, n_gather, HEAD_DIM, PAGE_SIZE), F32),
                            pltpu.SemaphoreType.DMA((2,)), pltpu.SemaphoreType.DMA((2,))]),
        out_shape=jax.ShapeDtypeStruct((n_samples, 1, ATTN_WIDTH), F32),
        compiler_params=pltpu.CompilerParams(dimension_semantics=("arbitrary",),
                                             vmem_limit_bytes=VMEM_LIMIT_DECODE),
        name="sample_moba",
    )(pt_flat, idx_flat, q_s, k_s, v_s, ck_t, cv_t)
    att_s = att_s.reshape(n_samples, ATTN_WIDTH).astype(BF16)

    y_sample = _back(xs, a_s, att_s, ga_s, gb_s, *back_w).reshape(n_samples, 1, D_MODEL)

    heads_t = lambda zt: jnp.transpose(zt.reshape(1, bsz, N_HEADS, HEAD_DIM, seq), (0, 1, 4, 2, 3))
    heads_s = lambda z: z.reshape(1, n_samples, 1, N_HEADS, HEAD_DIM)
    return (y_prompt, y_sample, heads_t(kt), heads_t(vt), heads_s(k_s), heads_s(v_s),
            vn_s.reshape(1, n_samples, 1, SGU_WIDTH))
```

```python
import functools

import numpy as np
import jax
import jax.numpy as jnp
from jax import lax
from jax.experimental import pallas as pl
from jax.experimental.pallas import tpu as pltpu

F32 = jnp.float32
BF16 = jnp.bfloat16

D_MODEL = 1024
N_HEADS = 8
HEAD_DIM = 64
ATTN_WIDTH = N_HEADS * HEAD_DIM
MOBA_BLOCK = 256
MOBA_TOPK = 3
CHUNK = 128
SGU_GROUPS = 4
SGU_WIDTH = 512
D_FF = 2816
PAGE_SIZE = 128
EPS = 1e-6
ALIBI_MAX_BIAS = 8.0

LANES = 128
PAIR = 2 * HEAD_DIM
N_PAIRS = N_HEADS // 2
NEG_BIG = -1e30
PAGES_PER_BLOCK = MOBA_BLOCK // PAGE_SIZE
SCALE = HEAD_DIM ** -0.5

TILE_FRONT = MOBA_BLOCK
TILE_BACK = 512
FF_CHUNK = 256
SELECT_RING = 16
SELECT_GROUP = 8

VMEM_LIMIT_FRONT = 44 * 1024 * 1024
VMEM_LIMIT_ATTN = 48 * 1024 * 1024
VMEM_LIMIT_BACK = 58 * 1024 * 1024
VMEM_LIMIT_DECODE = 32 * 1024 * 1024


def _slopes():
    return [2.0 ** (-(ALIBI_MAX_BIAS / N_HEADS) * (h + 1)) for h in range(N_HEADS)]


def _rms(x, g):
    return x * lax.rsqrt(jnp.mean(x * x, axis=-1, keepdims=True) + EPS) * g


def _split_bf16(x):
    hi = x.astype(BF16)
    lo = (x - hi.astype(F32)).astype(BF16)
    return hi, lo


def _dot_nt_f32(a, b):
    dn = (((1,), (1,)), ((), ()))
    ah, al = _split_bf16(a)
    bh, bl = _split_bf16(b)
    d = functools.partial(lax.dot_general, dimension_numbers=dn, preferred_element_type=F32)
    return d(ah, bh) + d(ah, bl) + d(al, bh)


def _resident(shape):
    nd = len(shape)
    return pl.BlockSpec(shape, lambda *_: (0,) * nd, pipeline_mode=pl.Buffered(1))


def _front_kernel(x_ref, gpre_ref, win_ref, gsgu_ref, wsp_ref, bspt_ref,
                  a_ref, qext_ref, kt_ref, vt_ref, vb_ref, ga_ref, gb_ref, ksum_ref):
    j = pl.program_id(1)

    @pl.when(j == 0)
    def _():
        ksum_ref[...] = jnp.zeros_like(ksum_ref)

    h = _rms(x_ref[0], gpre_ref[...]).astype(BF16)

    def proj(lo, hi):
        return jnp.dot(h, win_ref[:, lo:hi], preferred_element_type=F32)

    c0 = SGU_WIDTH
    c1 = 2 * SGU_WIDTH
    c2 = c1 + ATTN_WIDTH
    c3 = c2 + ATTN_WIDTH
    c4 = c3 + ATTN_WIDTH
    c5 = c4 + D_MODEL
    u = proj(0, c0)
    v = proj(c0, c1)
    q = proj(c1, c2)
    k = proj(c2, c3)
    vv = proj(c3, c4)
    ga_ref[0] = proj(c4, c5)
    gb_ref[0] = proj(c5, c5 + D_MODEL)

    vn = _rms(v, gsgu_ref[...]).astype(BF16)
    tri_r = lax.broadcasted_iota(jnp.int32, (CHUNK, CHUNK), 0)
    tri_c = lax.broadcasted_iota(jnp.int32, (CHUNK, CHUNK), 1)
    for g in range(SGU_GROUPS):
        wg = jnp.where(tri_c <= tri_r, wsp_ref[g], 0.0).astype(BF16)
        cols = slice(g * CHUNK, (g + 1) * CHUNK)
        for c in range(TILE_FRONT // CHUNK):
            rows = slice(c * CHUNK, (c + 1) * CHUNK)
            sv = jnp.dot(wg, vn[rows, cols], preferred_element_type=F32) + bspt_ref[:, g:g + 1]
            a_ref[0, rows, cols] = (u[rows, cols] * sv).astype(BF16)

    kt_ref[0] = k.T
    vt_ref[0] = vv.T
    vb_ref[0] = vv.astype(BF16)

    ksum_ref[pl.ds(j, 1), :] = jnp.sum(k, axis=0, keepdims=True)
    km = ksum_ref[...] * (1.0 / MOBA_BLOCK)
    n_blk = km.shape[0]
    km_rep = jnp.concatenate([km] * N_HEADS + [jnp.zeros((LANES - N_HEADS * n_blk, ATTN_WIDTH), F32)], axis=0)
    r_i = lax.broadcasted_iota(jnp.int32, (LANES, ATTN_WIDTH), 0)
    c_i = lax.broadcasted_iota(jnp.int32, (LANES, ATTN_WIDTH), 1)
    head_rows = (r_i // n_blk == c_i // HEAD_DIM) & (r_i < N_HEADS * n_blk)
    km_bd = jnp.where(head_rows, km_rep, 0.0)
    gate_t = _dot_nt_f32(km_bd, q)

    blk_i = lax.broadcasted_iota(jnp.int32, (n_blk, TILE_FRONT), 0)
    cand = blk_i < j
    bias_rows = []
    for hh in range(N_HEADS):
        gm = jnp.where(cand, gate_t[hh * n_blk:(hh + 1) * n_blk, :], -jnp.inf)
        rank = jnp.zeros((n_blk, TILE_FRONT), F32)
        for m in range(n_blk):
            gmm = gm[m:m + 1, :]
            ahead = (gmm > gm) | ((gmm == gm) & (blk_i > m))
            rank = rank + jnp.where(ahead, 1.0, 0.0)
        keep = (cand & (rank < float(MOBA_TOPK))) | (blk_i == j)
        bias_rows.append(jnp.where(keep, 0.0, NEG_BIG))
    bias_rows.append(jnp.full((LANES - N_HEADS * n_blk, TILE_FRONT), NEG_BIG, F32))
    sel = jnp.concatenate(bias_rows, axis=0).T
    sel_sw = pltpu.roll(sel, LANES // 2, axis=1)

    lane = lax.broadcasted_iota(jnp.int32, (TILE_FRONT, LANES), 1)
    for hh in range(N_HEADS):
        p = hh // 2
        qp = q[:, p * PAIR:(p + 1) * PAIR] * SCALE
        if hh % 2 == 0:
            lo = HEAD_DIM + hh * n_blk
            ext = jnp.where((lane >= lo) & (lane < lo + n_blk), sel_sw,
                            jnp.where(lane >= LANES - 2, 1.0, 0.0))
            qx = jnp.where(lane < HEAD_DIM, qp, ext)
        else:
            lo = hh * n_blk
            ext = jnp.where((lane >= lo) & (lane < lo + n_blk), sel,
                            jnp.where(lane < 2, 1.0, 0.0))
            qx = jnp.where(lane >= HEAD_DIM, qp, ext)
        qext_ref[0, hh] = qx.astype(BF16)


def _key_ext_const(seq):
    n_blk = seq // MOBA_BLOCK
    pos = np.arange(seq)
    off = (pos % MOBA_BLOCK).astype(np.float32)
    blk = pos // MOBA_BLOCK
    out = np.zeros((N_HEADS, PAIR, seq), np.float32)
    for hh, slope in enumerate(_slopes()):
        if hh % 2 == 0:
            base, r_off, r_blk = HEAD_DIM + hh * n_blk, PAIR - 2, PAIR - 1
        else:
            base, r_off, r_blk = hh * n_blk, 0, 1
        for n in range(n_blk):
            out[hh, base + n] = (blk == n)
        out[hh, r_off] = slope * off
        out[hh, r_blk] = slope * MOBA_BLOCK * blk
    return jnp.asarray(out, dtype=BF16)


def _attn_kernel(pt_ref, qext_ref, kt_ref, vb_ref, kext_ref, qs_ref, ks_ref, ck_ref, o_ref, idx_ref,
                 kx_ref, buf_ref, sem_ref, qb_ref, stage_ref, *, scorer_cfg):
    seq = kt_ref.shape[2]
    n_blk = seq // MOBA_BLOCK
    step = pl.program_id(0) * pl.num_programs(1) + pl.program_id(1)
    scorer = _PagedScorer(pt_ref, qs_ref, ks_ref, ck_ref, idx_ref, buf_ref, sem_ref, qb_ref, stage_ref,
                          step=step, **scorer_cfg)
    assert scorer.n_groups <= n_blk
    scorer.begin()

    kt = kt_ref[0].astype(BF16)
    row = lax.broadcasted_iota(jnp.int32, (PAIR, seq), 0)
    kx_ref[0] = jnp.where(row < HEAD_DIM, kt, kext_ref[0])
    kx_ref[1] = jnp.where(row >= HEAD_DIM, kt, kext_ref[1])

    r_i = lax.broadcasted_iota(jnp.int32, (MOBA_BLOCK, MOBA_BLOCK), 0)
    c_i = lax.broadcasted_iota(jnp.int32, (MOBA_BLOCK, MOBA_BLOCK), 1)
    causal = c_i <= r_i
    lane = lax.broadcasted_iota(jnp.int32, (MOBA_BLOCK, PAIR), 1)
    for j in range(n_blk):
        grp = j - (n_blk - scorer.n_groups)
        riding = grp >= 0
        if riding:
            scorer.wait(grp)
            zeros = scorer.score(grp)
        w = (j + 1) * MOBA_BLOCK
        rows = slice(j * MOBA_BLOCK, w)
        outs = []
        for e in range(2):
            s = jnp.dot(qext_ref[0, e, rows, :], kx_ref[e, :, :w], preferred_element_type=F32)
            own = jnp.where(causal, s[:, w - MOBA_BLOCK:], NEG_BIG)
            s = own if j == 0 else jnp.concatenate([s[:, :w - MOBA_BLOCK], own], axis=1)
            m = jnp.max(s, axis=-1, keepdims=True)
            p = jnp.exp(s - m)
            l = jnp.sum(p, axis=-1, keepdims=True)
            if riding and e == 1:
                p = _add_zero_rows(p, zeros)
            o = jnp.dot(p.astype(BF16), vb_ref[0, :w, :], preferred_element_type=F32)
            outs.append(o / l)
        o_ref[0, rows, :] = jnp.where(lane < HEAD_DIM, outs[0], outs[1]).astype(BF16)
        if riding:
            scorer.refill(grp)
    scorer.finish()


def _back_kernel(*refs, scorer_cfg):
    if scorer_cfg is None:
        scorer = None
        (x_ref, a_ref, att_ref, ga_ref, gb_ref, wa_ref, wb_ref, wo_ref, gpm_ref, gpf_ref,
         wg_ref, wu_ref, wd_ref, gpo_ref, o_ref) = refs
    else:
        (pt_ref, x_ref, a_ref, att_ref, ga_ref, gb_ref, wa_ref, wb_ref, wo_ref, gpm_ref, gpf_ref,
         wg_ref, wu_ref, wd_ref, gpo_ref, qs_ref, ks_ref, ck_ref, o_ref, idx_ref,
         buf_ref, sem_ref, qb_ref, stage_ref) = refs
        scorer = _PagedScorer(pt_ref, qs_ref, ks_ref, ck_ref, idx_ref, buf_ref, sem_ref, qb_ref, stage_ref,
                              step=pl.program_id(0), **scorer_cfg)
        assert scorer.n_groups <= D_FF // FF_CHUNK
        scorer.begin()
    ya = jnp.dot(a_ref[...], wa_ref[...], preferred_element_type=F32)
    yb = jnp.dot(att_ref[...], wb_ref[...], preferred_element_type=F32)
    m = jax.nn.sigmoid(ga_ref[...]) * ya + jax.nn.sigmoid(gb_ref[...]) * yb
    y = jnp.dot(m.astype(BF16), wo_ref[...], preferred_element_type=F32)
    x1 = x_ref[...] + _rms(y, gpm_ref[...])
    h2 = _rms(x1, gpf_ref[...]).astype(BF16)
    acc = jnp.zeros(x1.shape, F32)
    n_chunks = D_FF // FF_CHUNK
    stride = 1 if scorer is None else n_chunks // scorer.n_groups
    for c in range(n_chunks):
        grp = c // stride
        riding = scorer is not None and c % stride == 0 and grp < scorer.n_groups
        if riding:
            scorer.wait(grp)
        cols = slice(c * FF_CHUNK, (c + 1) * FF_CHUNK)
        g = jnp.dot(h2, wg_ref[:, cols], preferred_element_type=F32)
        up = jnp.dot(h2, wu_ref[:, cols], preferred_element_type=F32)
        act = (g * jax.nn.sigmoid(g)) * up
        if riding:
            act = _add_zero_rows(act, scorer.score(grp))
        acc = acc + jnp.dot(act.astype(BF16), wd_ref[cols, :], preferred_element_type=F32)
        if riding:
            scorer.refill(grp)
    o_ref[...] = x1 + _rms(acc, gpo_ref[...])
    if scorer is not None:
        scorer.finish()


def _add_zero_rows(x, zeros):
    r, c = zeros.shape
    z = jnp.concatenate([zeros] * (x.shape[1] // c), axis=1)
    return jnp.concatenate([x[:r] + z, x[r:]], axis=0)


def _scorer_scratch():
    return [pltpu.VMEM((SELECT_RING, PAGES_PER_BLOCK, ATTN_WIDTH, PAGE_SIZE), F32),
            pltpu.SemaphoreType.DMA((SELECT_RING,)),
            pltpu.VMEM((ATTN_WIDTH, PAGE_SIZE), F32),
            pltpu.VMEM((SELECT_GROUP, HEAD_DIM, PAGE_SIZE), F32)]


def _back(x, a, att, ga, gb, w_a, w_b, w_o, gpm, gpf, w_g, w_u, w_d, gpo, paged=None):
    n = x.shape[0]
    tm = min(TILE_BACK, n)
    steps = n // tm
    row = lambda w: pl.BlockSpec((tm, w), lambda i, *_: (i, 0))
    in_specs = [row(D_MODEL), row(SGU_WIDTH), row(ATTN_WIDTH), row(D_MODEL), row(D_MODEL),
                _resident(w_a.shape), _resident(w_b.shape), _resident(w_o.shape),
                _resident(gpm.shape), _resident(gpf.shape),
                _resident(w_g.shape), _resident(w_u.shape), _resident(w_d.shape), _resident(gpo.shape)]
    args = (x, a, att, ga, gb, w_a, w_b, w_o, gpm, gpf, w_g, w_u, w_d, gpo)
    params = pltpu.CompilerParams(dimension_semantics=("arbitrary",), vmem_limit_bytes=VMEM_LIMIT_BACK)
    if paged is None:
        return pl.pallas_call(
            functools.partial(_back_kernel, scorer_cfg=None),
            grid=(steps,), in_specs=in_specs, out_specs=row(D_MODEL),
            out_shape=jax.ShapeDtypeStruct((n, D_MODEL), F32),
            compiler_params=params, name="mixer_back_ffn",
        )(*args)
    pt_flat, q_s, k_s, ck_pages, cfg = paged
    assert cfg["n_local"] == steps
    return pl.pallas_call(
        functools.partial(_back_kernel, scorer_cfg=cfg),
        grid_spec=pltpu.PrefetchScalarGridSpec(
            num_scalar_prefetch=1, grid=(steps,),
            in_specs=in_specs + [_resident(q_s.shape), _resident(k_s.shape), pl.BlockSpec(memory_space=pl.ANY)],
            out_specs=[row(D_MODEL), pl.BlockSpec((1, N_HEADS, LANES), lambda i, *_: (i, 0, 0))],
            scratch_shapes=_scorer_scratch()),
        out_shape=[jax.ShapeDtypeStruct((n, D_MODEL), F32),
                   jax.ShapeDtypeStruct((steps, N_HEADS, LANES), jnp.int32)],
        compiler_params=params, name="mixer_back_ffn_scoring",
    )(pt_flat, *args, q_s, k_s, ck_pages)


def _front_sample_kernel(x_ref, gpre_ref, win_ref, gsgu_ref, coef_ref, bias_ref,
                         a_ref, vn_ref, q_ref, k_ref, v_ref, ga_ref, gb_ref):
    h = _rms(x_ref[...], gpre_ref[...]).astype(BF16)

    def proj(lo, hi):
        return jnp.dot(h, win_ref[:, lo:hi], preferred_element_type=F32)

    c0 = SGU_WIDTH
    c1 = 2 * SGU_WIDTH
    c2 = c1 + ATTN_WIDTH
    c3 = c2 + ATTN_WIDTH
    c4 = c3 + ATTN_WIDTH
    c5 = c4 + D_MODEL
    u = proj(0, c0)
    vn = _rms(proj(c0, c1), gsgu_ref[...])
    vn_ref[...] = vn
    a_ref[...] = (u * (vn * coef_ref[...] + bias_ref[...])).astype(BF16)
    q_ref[...] = proj(c1, c2)
    k_ref[...] = proj(c2, c3)
    v_ref[...] = proj(c3, c4)
    ga_ref[...] = proj(c4, c5)
    gb_ref[...] = proj(c5, c5 + D_MODEL)


def _lane_bcast_col(row, n_rows):
    return jnp.broadcast_to(row, (n_rows, row.shape[1])).T


def _head_sums(row):
    return [jnp.sum(row[:, h * HEAD_DIM:(h + 1) * HEAD_DIM], axis=1, keepdims=True) for h in range(N_HEADS)]


class _PagedScorer:
    def __init__(self, pt_ref, q_ref, k_ref, ck_ref, idx_ref, buf_ref, sem_ref, qb_ref, stage_ref, *,
                 step, first_sample, n_local, n_pages, n_past_blk):
        self.pt_ref, self.q_ref, self.k_ref, self.ck_ref, self.idx_ref = pt_ref, q_ref, k_ref, ck_ref, idx_ref
        self.buf_ref, self.sem_ref, self.qb_ref, self.stage_ref = buf_ref, sem_ref, qb_ref, stage_ref
        self.step, self.n_local, self.n_pages, self.n_past_blk = step, n_local, n_pages, n_past_blk
        self.sample = first_sample + step
        self.first_sample = first_sample
        self.n_groups = n_past_blk // SELECT_GROUP

    def _copies(self, sample, blk):
        slot = blk % SELECT_RING
        base = sample * self.n_pages + blk * PAGES_PER_BLOCK
        return [pltpu.make_async_copy(self.ck_ref.at[self.pt_ref[base + pg]], self.buf_ref.at[slot, pg],
                                      self.sem_ref.at[slot]) for pg in range(PAGES_PER_BLOCK)]

    def begin(self):
        @pl.when(self.step == 0)
        def _():
            for blk in range(SELECT_RING):
                for cp in self._copies(self.first_sample, blk):
                    cp.start()

        q_row = self.q_ref[pl.ds(self.sample, 1), :]
        self.qb_ref[...] = _lane_bcast_col(q_row, PAGE_SIZE)
        self.gate = jnp.zeros((N_HEADS, LANES), F32)

    def wait(self, g):
        for blk in range(g * SELECT_GROUP, (g + 1) * SELECT_GROUP):
            for cp in self._copies(self.sample, blk):
                cp.wait()

    def score(self, g):
        lane = lax.broadcasted_iota(jnp.int32, (N_HEADS, LANES), 1)
        blks = list(range(g * SELECT_GROUP, (g + 1) * SELECT_GROUP))
        sub = 8
        for h in range(N_HEADS):
            accs = [None] * len(blks)
            for r in range(h * HEAD_DIM, (h + 1) * HEAD_DIM, sub):
                q8 = self.qb_ref[r:r + sub, :]
                for i, blk in enumerate(blks):
                    slot = blk % SELECT_RING
                    t = (self.buf_ref[slot, 0, r:r + sub, :] + self.buf_ref[slot, 1, r:r + sub, :]) * q8
                    accs[i] = t if accs[i] is None else accs[i] + t
            for i in range(len(blks)):
                self.stage_ref[i, h * sub:(h + 1) * sub, :] = accs[i]
        for i, blk in enumerate(blks):
            s_blk = self.stage_ref[i, pl.ds(0, N_HEADS, stride=sub), :]
            for d in range(1, sub):
                s_blk = s_blk + self.stage_ref[i, pl.ds(d, N_HEADS, stride=sub), :]
            gsum = jnp.sum(s_blk, axis=1, keepdims=True)
            self.gate = jnp.where(lane == blk, gsum, self.gate)
        bits = lax.bitcast_convert_type(self.gate, jnp.uint32)
        sixteen = jnp.uint32(16)
        return lax.shift_right_logical(lax.shift_right_logical(bits, sixteen), sixteen).astype(F32)

    def refill(self, g):
        nxt = g * SELECT_GROUP + SELECT_RING
        if nxt < self.n_past_blk:
            for i in range(SELECT_GROUP):
                for cp in self._copies(self.sample, nxt + i):
                    cp.start()
        else:
            @pl.when(self.step + 1 < self.n_local)
            def _():
                for i in range(SELECT_GROUP):
                    for cp in self._copies(self.sample + 1, nxt - self.n_past_blk + i):
                        cp.start()

    def finish(self):
        lane = lax.broadcasted_iota(jnp.int32, (N_HEADS, LANES), 1)
        sub = lax.broadcasted_iota(jnp.int32, (N_HEADS, LANES), 0)
        gate = self.gate
        q_row = self.q_ref[pl.ds(self.sample, 1), :]
        own = _head_sums(q_row * self.k_ref[pl.ds(self.sample, 1), :])
        for h in range(N_HEADS):
            gate = jnp.where((lane == self.n_past_blk) & (sub == h), own[h], gate)
        gate = gate * (1.0 / MOBA_BLOCK)
        gm = jnp.where(lane < self.n_past_blk, gate, -jnp.inf)
        lane_f = lane.astype(F32)
        out = jnp.zeros((N_HEADS, LANES), F32)
        for r in range(MOBA_TOPK):
            mx = jnp.max(gm, axis=1, keepdims=True)
            pick = jnp.min(jnp.where(gm == mx, lane_f, float(LANES)), axis=1, keepdims=True)
            out = jnp.where(lane == r, pick, out)
            gm = jnp.where(lane_f == pick, -jnp.inf, gm)
        self.idx_ref[0] = out.astype(jnp.int32)


def _decode_attn_kernel(pt_ref, idx_ref, q_ref, k_ref, v_ref, ck_ref, cv_ref, o_ref,
                        kbuf_ref, vbuf_ref, ksem_ref, vsem_ref, *, n_samples, n_pages, past_len):
    n = pl.program_id(0)
    n_tiles = MOBA_TOPK * PAGES_PER_BLOCK
    per_sample = N_HEADS * n_tiles

    def copies(sample, slot):
        out = []
        for h in range(N_HEADS):
            for j in range(MOBA_TOPK):
                blk = idx_ref[sample * (N_HEADS * MOBA_TOPK) + h * MOBA_TOPK + j]
                for pg in range(PAGES_PER_BLOCK):
                    phys = pt_ref[sample * n_pages + blk * PAGES_PER_BLOCK + pg]
                    i = (h * MOBA_TOPK + j) * PAGES_PER_BLOCK + pg
                    out.append(pltpu.make_async_copy(ck_ref.at[phys, h], kbuf_ref.at[slot, i], ksem_ref.at[slot]))
                    out.append(pltpu.make_async_copy(cv_ref.at[phys, h], vbuf_ref.at[slot, i], vsem_ref.at[slot]))
        return out

    @pl.when(n == 0)
    def _():
        for cp in copies(0, 0):
            cp.start()

    @pl.when(n + 1 < n_samples)
    def _():
        for cp in copies(n + 1, (n + 1) % 2):
            cp.start()

    slot = n % 2
    for i in range(per_sample):
        pltpu.make_async_copy(ck_ref.at[0, 0], kbuf_ref.at[slot, i], ksem_ref.at[slot]).wait()
        pltpu.make_async_copy(cv_ref.at[0, 0], vbuf_ref.at[slot, i], vsem_ref.at[slot]).wait()

    q_row = q_ref[pl.ds(n, 1), :]
    k_row = k_ref[pl.ds(n, 1), :]
    v_row = v_ref[pl.ds(n, 1), :]
    qb = _lane_bcast_col(q_row, PAGE_SIZE)
    s_new = _head_sums(q_row * k_row)
    lane_row = lax.broadcasted_iota(jnp.int32, (1, PAGE_SIZE), 1)
    lane_w = lax.broadcasted_iota(jnp.int32, (1, ATTN_WIDTH), 1)
    acc_heads = []
    p_new_row = jnp.zeros((1, ATTN_WIDTH), F32)
    for h, slope in enumerate(_slopes()):
        qh = qb[h * HEAD_DIM:(h + 1) * HEAD_DIM, :]
        scores = []
        for j in range(MOBA_TOPK):
            blk = idx_ref[n * (N_HEADS * MOBA_TOPK) + h * MOBA_TOPK + j]
            for pg in range(PAGES_PER_BLOCK):
                i = (h * MOBA_TOPK + j) * PAGES_PER_BLOCK + pg
                s = jnp.sum(kbuf_ref[slot, i] * qh, axis=0, keepdims=True) * SCALE
                key_pos = blk * MOBA_BLOCK + pg * PAGE_SIZE + lane_row
                scores.append(s - slope * (past_len - key_pos).astype(F32))
        s_n = s_new[h] * SCALE
        m = s_n
        for s in scores:
            m = jnp.maximum(m, jnp.max(s, axis=1, keepdims=True))
        p_n = jnp.exp(s_n - m)
        l = p_n
        acc = jnp.zeros((HEAD_DIM, PAGE_SIZE), F32)
        for t, s in enumerate(scores):
            p = jnp.exp(s - m)
            l = l + jnp.sum(p, axis=1, keepdims=True)
            acc = acc + vbuf_ref[slot, h * n_tiles + t] * p
        acc_heads.append(acc / l)
        p_new_row = jnp.where(lane_w // HEAD_DIM == h, p_n / l, p_new_row)
    acc_all = jnp.concatenate(acc_heads, axis=0)
    o_row = jnp.sum(acc_all.T, axis=0, keepdims=True) + p_new_row * v_row
    o_ref[0] = o_row


def kernel(x_prompt, x_sample, cache_k, cache_v, page_table, g_pre_mix, w_in, g_sgu, w_spatial, b_spatial,
           w_a, w_b, w_o, g_post_mix, g_pre_ffn, w_gate, w_up, w_down, g_post_ffn):
    depth = w_in.shape[0]
    assert depth == 1, "kernels are written for a single layer"
    bsz, seq, _ = x_prompt.shape
    n_samples, dec_seq, _ = x_sample.shape
    assert dec_seq == 1
    n_pool = cache_k.shape[1]
    n_pages = page_table.shape[1]
    past_len = n_pages * PAGE_SIZE
    n_past_blk = past_len // MOBA_BLOCK
    n_blk = seq // MOBA_BLOCK
    assert n_blk * N_HEADS <= HEAD_DIM and n_past_blk < LANES
    assert n_past_blk % SELECT_RING == 0 and SELECT_RING % SELECT_GROUP == 0

    l = 0
    w_in_b = w_in[l].astype(BF16)
    back_w = (w_a[l].astype(BF16), w_b[l].astype(BF16), w_o[l].astype(BF16), g_post_mix[l][None], g_pre_ffn[l][None],
              w_gate[l].astype(BF16), w_up[l].astype(BF16), w_down[l].astype(BF16), g_post_ffn[l][None])
    gpre = g_pre_mix[l][None]
    gsgu = g_sgu[l][None]
    in_cols = w_in_b.shape[1]

    xs = x_sample.reshape(n_samples, D_MODEL)
    coef = jnp.repeat(w_spatial[l, :, 0, 0], CHUNK)[None]
    bias = jnp.repeat(b_spatial[l, :, 0], CHUNK)[None]
    full = lambda shape: pl.BlockSpec(shape, lambda *_: (0,) * len(shape))
    s_w = lambda w, dt: jax.ShapeDtypeStruct((n_samples, w), dt)
    a_s, vn_s, q_s, k_s, v_s, ga_s, gb_s = pl.pallas_call(
        _front_sample_kernel,
        grid=(1,),
        in_specs=[full((n_samples, D_MODEL)), full((1, D_MODEL)), full((D_MODEL, in_cols)), full((1, SGU_WIDTH)),
                  full((1, SGU_WIDTH)), full((1, SGU_WIDTH))],
        out_specs=[full((n_samples, SGU_WIDTH)), full((n_samples, SGU_WIDTH)), full((n_samples, ATTN_WIDTH)),
                   full((n_samples, ATTN_WIDTH)), full((n_samples, ATTN_WIDTH)),
                   full((n_samples, D_MODEL)), full((n_samples, D_MODEL))],
        out_shape=[s_w(SGU_WIDTH, BF16), s_w(SGU_WIDTH, F32), s_w(ATTN_WIDTH, F32), s_w(ATTN_WIDTH, F32),
                   s_w(ATTN_WIDTH, F32), s_w(D_MODEL, F32), s_w(D_MODEL, F32)],
        compiler_params=pltpu.CompilerParams(dimension_semantics=("arbitrary",),
                                             vmem_limit_bytes=VMEM_LIMIT_FRONT),
        name="sample_front",
    )(xs, gpre, w_in_b, gsgu, coef, bias)

    ck_t = jnp.transpose(cache_k[l], (0, 2, 3, 1))
    cv_t = jnp.transpose(cache_v[l], (0, 2, 3, 1))
    ck_pages = ck_t.reshape(n_pool, ATTN_WIDTH, PAGE_SIZE)
    pt_flat = page_table.reshape(-1)
    n_tok = bsz * seq
    steps_attn = bsz * N_PAIRS
    steps_back = n_tok // TILE_BACK
    assert steps_attn + steps_back == n_samples
    scorer_cfg = lambda first, count: dict(first_sample=first, n_local=count, n_pages=n_pages, n_past_blk=n_past_blk)

    n_tiles = seq // TILE_FRONT
    tile3 = lambda w: pl.BlockSpec((1, TILE_FRONT, w), lambda b, j: (b, j, 0))
    tile3t = lambda w: pl.BlockSpec((1, w, TILE_FRONT), lambda b, j: (b, 0, j))
    a_p, qext, kt, vt, vb, ga, gb = pl.pallas_call(
        _front_kernel,
        grid=(bsz, n_tiles),
        in_specs=[tile3(D_MODEL), _resident((1, D_MODEL)), _resident((D_MODEL, in_cols)), _resident((1, SGU_WIDTH)),
                  _resident((SGU_GROUPS, CHUNK, CHUNK)), _resident((CHUNK, SGU_GROUPS))],
        out_specs=[tile3(SGU_WIDTH),
                   pl.BlockSpec((1, N_HEADS, TILE_FRONT, PAIR), lambda b, j: (b, 0, j, 0)),
                   tile3t(ATTN_WIDTH), tile3t(ATTN_WIDTH), tile3(ATTN_WIDTH), tile3(D_MODEL), tile3(D_MODEL)],
        out_shape=[jax.ShapeDtypeStruct((bsz, seq, SGU_WIDTH), BF16),
                   jax.ShapeDtypeStruct((bsz, N_HEADS, seq, PAIR), BF16),
                   jax.ShapeDtypeStruct((bsz, ATTN_WIDTH, seq), F32),
                   jax.ShapeDtypeStruct((bsz, ATTN_WIDTH, seq), F32),
                   jax.ShapeDtypeStruct((bsz, seq, ATTN_WIDTH), BF16),
                   jax.ShapeDtypeStruct((bsz, seq, D_MODEL), F32),
                   jax.ShapeDtypeStruct((bsz, seq, D_MODEL), F32)],
        scratch_shapes=[pltpu.VMEM((n_blk, ATTN_WIDTH), F32)],
        compiler_params=pltpu.CompilerParams(dimension_semantics=("arbitrary", "arbitrary"),
                                             vmem_limit_bytes=VMEM_LIMIT_FRONT),
        name="prompt_front",
    )(x_prompt, gpre, w_in_b, gsgu, w_spatial[l], b_spatial[l].T)

    kext = _key_ext_const(seq)
    att_p, idx_a = pl.pallas_call(
        functools.partial(_attn_kernel, scorer_cfg=scorer_cfg(0, steps_attn)),
        grid_spec=pltpu.PrefetchScalarGridSpec(
            num_scalar_prefetch=1,
            grid=(bsz, N_PAIRS),
            in_specs=[pl.BlockSpec((1, 2, seq, PAIR), lambda b, p, pt: (b, p, 0, 0)),
                      pl.BlockSpec((1, PAIR, seq), lambda b, p, pt: (b, p, 0)),
                      pl.BlockSpec((1, seq, PAIR), lambda b, p, pt: (b, 0, p)),
                      pl.BlockSpec((2, PAIR, seq), lambda b, p, pt: (p, 0, 0)),
                      _resident(q_s.shape), _resident(k_s.shape), pl.BlockSpec(memory_space=pl.ANY)],
            out_specs=[pl.BlockSpec((1, seq, PAIR), lambda b, p, pt: (b, 0, p)),
                       pl.BlockSpec((1, N_HEADS, LANES), lambda b, p, pt: (b * N_PAIRS + p, 0, 0))],
            scratch_shapes=[pltpu.VMEM((2, PAIR, seq), BF16)] + _scorer_scratch()),
        out_shape=[jax.ShapeDtypeStruct((bsz, seq, ATTN_WIDTH), BF16),
                   jax.ShapeDtypeStruct((steps_attn, N_HEADS, LANES), jnp.int32)],
        compiler_params=pltpu.CompilerParams(dimension_semantics=("arbitrary", "arbitrary"),
                                             vmem_limit_bytes=VMEM_LIMIT_ATTN),
        name="prompt_moba_scoring",
    )(pt_flat, qext, kt, vb, kext, q_s, k_s, ck_pages)

    flat = lambda z: z.reshape(n_tok, z.shape[-1])
    y_prompt, idx_b = _back(flat(x_prompt), flat(a_p), flat(att_p), flat(ga), flat(gb), *back_w,
                            paged=(pt_flat, q_s, k_s, ck_pages, scorer_cfg(steps_attn, steps_back)))
    y_prompt = y_prompt.reshape(bsz, seq, D_MODEL)

    idx_flat = jnp.concatenate([idx_a, idx_b], axis=0)[:, :, :MOBA_TOPK].reshape(-1)
    n_gather = N_HEADS * MOBA_TOPK * PAGES_PER_BLOCK
    att_s = pl.pallas_call(
        functools.partial(_decode_attn_kernel, n_samples=n_samples, n_pages=n_pages, past_len=past_len),
        grid_spec=pltpu.PrefetchScalarGridSpec(
            num_scalar_prefetch=2,
            grid=(n_samples,),
            in_specs=[pl.BlockSpec((n_samples, ATTN_WIDTH), lambda i, pt, ix: (0, 0)),
                      pl.BlockSpec((n_samples, ATTN_WIDTH), lambda i, pt, ix: (0, 0)),
                      pl.BlockSpec((n_samples, ATTN_WIDTH), lambda i, pt, ix: (0, 0)),
                      pl.BlockSpec(memory_space=pl.ANY), pl.BlockSpec(memory_space=pl.ANY)],
            out_specs=pl.BlockSpec((1, 1, ATTN_WIDTH), lambda i, pt, ix: (i, 0, 0)),
            scratch_shapes=[pltpu.VMEM((2, n_gather, HEAD_DIM, PAGE_SIZE), F32),
                            pltpu.VMEM((2, n_gather, HEAD_DIM, PAGE_SIZE), F32),
                            pltpu.SemaphoreType.DMA((2,)), pltpu.SemaphoreType.DMA((2,))]),
        out_shape=jax.ShapeDtypeStruct((n_samples, 1, ATTN_WIDTH), F32),
        compiler_params=pltpu.CompilerParams(dimension_semantics=("arbitrary",),
                                             vmem_limit_bytes=VMEM_LIMIT_DECODE),
        name="sample_moba",
    )(pt_flat, idx_flat, q_s, k_s, v_s, ck_t, cv_t)
    att_s = att_s.reshape(n_samples, ATTN_WIDTH).astype(BF16)

    y_sample = _back(xs, a_s, att_s, ga_s, gb_s, *back_w).reshape(n_samples, 1, D_MODEL)

    heads_t = lambda zt: jnp.transpose(zt.reshape(1, bsz, N_HEADS, HEAD_DIM, seq), (0, 1, 4, 2, 3))
    heads_s = lambda z: z.reshape(1, n_samples, 1, N_HEADS, HEAD_DIM)
    return (y_prompt, y_sample, heads_t(kt), heads_t(vt), heads_s(k_s), heads_s(v_s),
            vn_s.reshape(1, n_samples, 1, SGU_WIDTH))
```

```python
import functools

import numpy as np
import jax
import jax.numpy as jnp
from jax import lax
from jax.experimental import pallas as pl
from jax.experimental.pallas import tpu as pltpu
from jax.experimental.pallas import tpu_sc as plsc

F32 = jnp.float32
BF16 = jnp.bfloat16

D_MODEL = 1024
N_HEADS = 8
HEAD_DIM = 64
ATTN_WIDTH = N_HEADS * HEAD_DIM
MOBA_BLOCK = 256
MOBA_TOPK = 3
CHUNK = 128
SGU_GROUPS = 4
SGU_WIDTH = 512
D_FF = 2816
PAGE_SIZE = 128
EPS = 1e-6
ALIBI_MAX_BIAS = 8.0

LANES = 128
PAIR = 2 * HEAD_DIM
N_PAIRS = N_HEADS // 2
NEG_BIG = -1e30
PAGES_PER_BLOCK = MOBA_BLOCK // PAGE_SIZE
SCALE = HEAD_DIM ** -0.5

TILE_FRONT = MOBA_BLOCK
TILE_BACK = 512
FF_CHUNK = 256
TOPK_PER_STEP = 8

SC_CORES = 2
SC_SUBCORES = 16
SC_WORKERS = SC_CORES * SC_SUBCORES
SC_LANES = 16
SC_SLAB_ROWS = 16
SC_CHUNKS_PER_PAGE = 2
SC_SLABS_PER_CHUNK = ATTN_WIDTH // SC_CHUNKS_PER_PAGE // SC_SLAB_ROWS
SC_ROW_UNROLL = 4

VMEM_LIMIT_FRONT = 44 * 1024 * 1024
VMEM_LIMIT_ATTN = 48 * 1024 * 1024
VMEM_LIMIT_BACK = 56 * 1024 * 1024
VMEM_LIMIT_DECODE = 32 * 1024 * 1024


def _slopes():
    return [2.0 ** (-(ALIBI_MAX_BIAS / N_HEADS) * (h + 1)) for h in range(N_HEADS)]


def _rms(x, g):
    return x * lax.rsqrt(jnp.mean(x * x, axis=-1, keepdims=True) + EPS) * g


def _split_bf16(x):
    hi = x.astype(BF16)
    lo = (x - hi.astype(F32)).astype(BF16)
    return hi, lo


def _dot_nt_f32(a, b):
    dn = (((1,), (1,)), ((), ()))
    ah, al = _split_bf16(a)
    bh, bl = _split_bf16(b)
    d = functools.partial(lax.dot_general, dimension_numbers=dn, preferred_element_type=F32)
    return d(ah, bh) + d(ah, bl) + d(al, bh)


def _resident(shape):
    nd = len(shape)
    return pl.BlockSpec(shape, lambda *_: (0,) * nd, pipeline_mode=pl.Buffered(1))


def _front_kernel(x_ref, gpre_ref, win_ref, gsgu_ref, wsp_ref, bspt_ref,
                  a_ref, qext_ref, kt_ref, vt_ref, vb_ref, ga_ref, gb_ref, ksum_ref):
    j = pl.program_id(1)

    @pl.when(j == 0)
    def _():
        ksum_ref[...] = jnp.zeros_like(ksum_ref)

    h = _rms(x_ref[0], gpre_ref[...]).astype(BF16)

    def proj(lo, hi):
        return jnp.dot(h, win_ref[:, lo:hi], preferred_element_type=F32)

    c0 = SGU_WIDTH
    c1 = 2 * SGU_WIDTH
    c2 = c1 + ATTN_WIDTH
    c3 = c2 + ATTN_WIDTH
    c4 = c3 + ATTN_WIDTH
    c5 = c4 + D_MODEL
    u = proj(0, c0)
    v = proj(c0, c1)
    q = proj(c1, c2)
    k = proj(c2, c3)
    vv = proj(c3, c4)
    ga_ref[0] = proj(c4, c5)
    gb_ref[0] = proj(c5, c5 + D_MODEL)

    vn = _rms(v, gsgu_ref[...]).astype(BF16)
    tri_r = lax.broadcasted_iota(jnp.int32, (CHUNK, CHUNK), 0)
    tri_c = lax.broadcasted_iota(jnp.int32, (CHUNK, CHUNK), 1)
    for g in range(SGU_GROUPS):
        wg = jnp.where(tri_c <= tri_r, wsp_ref[g], 0.0).astype(BF16)
        cols = slice(g * CHUNK, (g + 1) * CHUNK)
        for c in range(TILE_FRONT // CHUNK):
            rows = slice(c * CHUNK, (c + 1) * CHUNK)
            sv = jnp.dot(wg, vn[rows, cols], preferred_element_type=F32) + bspt_ref[:, g:g + 1]
            a_ref[0, rows, cols] = (u[rows, cols] * sv).astype(BF16)

    kt_ref[0] = k.T
    vt_ref[0] = vv.T
    vb_ref[0] = vv.astype(BF16)

    ksum_ref[pl.ds(j, 1), :] = jnp.sum(k, axis=0, keepdims=True)
    km = ksum_ref[...] * (1.0 / MOBA_BLOCK)
    n_blk = km.shape[0]
    km_rep = jnp.concatenate([km] * N_HEADS + [jnp.zeros((LANES - N_HEADS * n_blk, ATTN_WIDTH), F32)], axis=0)
    r_i = lax.broadcasted_iota(jnp.int32, (LANES, ATTN_WIDTH), 0)
    c_i = lax.broadcasted_iota(jnp.int32, (LANES, ATTN_WIDTH), 1)
    head_rows = (r_i // n_blk == c_i // HEAD_DIM) & (r_i < N_HEADS * n_blk)
    km_bd = jnp.where(head_rows, km_rep, 0.0)
    gate_t = _dot_nt_f32(km_bd, q)

    blk_i = lax.broadcasted_iota(jnp.int32, (n_blk, TILE_FRONT), 0)
    cand = blk_i < j
    bias_rows = []
    for hh in range(N_HEADS):
        gm = jnp.where(cand, gate_t[hh * n_blk:(hh + 1) * n_blk, :], -jnp.inf)
        rank = jnp.zeros((n_blk, TILE_FRONT), F32)
        for m in range(n_blk):
            gmm = gm[m:m + 1, :]
            ahead = (gmm > gm) | ((gmm == gm) & (blk_i > m))
            rank = rank + jnp.where(ahead, 1.0, 0.0)
        keep = (cand & (rank < float(MOBA_TOPK))) | (blk_i == j)
        bias_rows.append(jnp.where(keep, 0.0, NEG_BIG))
    bias_rows.append(jnp.full((LANES - N_HEADS * n_blk, TILE_FRONT), NEG_BIG, F32))
    sel = jnp.concatenate(bias_rows, axis=0).T
    sel_sw = pltpu.roll(sel, LANES // 2, axis=1)

    lane = lax.broadcasted_iota(jnp.int32, (TILE_FRONT, LANES), 1)
    for hh in range(N_HEADS):
        p = hh // 2
        qp = q[:, p * PAIR:(p + 1) * PAIR] * SCALE
        if hh % 2 == 0:
            lo = HEAD_DIM + hh * n_blk
            ext = jnp.where((lane >= lo) & (lane < lo + n_blk), sel_sw,
                            jnp.where(lane >= LANES - 2, 1.0, 0.0))
            qx = jnp.where(lane < HEAD_DIM, qp, ext)
        else:
            lo = hh * n_blk
            ext = jnp.where((lane >= lo) & (lane < lo + n_blk), sel,
                            jnp.where(lane < 2, 1.0, 0.0))
            qx = jnp.where(lane >= HEAD_DIM, qp, ext)
        qext_ref[0, hh] = qx.astype(BF16)


def _key_ext_const(seq):
    n_blk = seq // MOBA_BLOCK
    pos = np.arange(seq)
    off = (pos % MOBA_BLOCK).astype(np.float32)
    blk = pos // MOBA_BLOCK
    out = np.zeros((N_HEADS, PAIR, seq), np.float32)
    for hh, slope in enumerate(_slopes()):
        if hh % 2 == 0:
            base, r_off, r_blk = HEAD_DIM + hh * n_blk, PAIR - 2, PAIR - 1
        else:
            base, r_off, r_blk = hh * n_blk, 0, 1
        for n in range(n_blk):
            out[hh, base + n] = (blk == n)
        out[hh, r_off] = slope * off
        out[hh, r_blk] = slope * MOBA_BLOCK * blk
    return jnp.asarray(out, dtype=BF16)


def _attn_kernel(qext_ref, kt_ref, vb_ref, kext_ref, o_ref, kx_ref):
    seq = kt_ref.shape[2]
    n_blk = seq // MOBA_BLOCK
    kt = kt_ref[0].astype(BF16)
    row = lax.broadcasted_iota(jnp.int32, (PAIR, seq), 0)
    kx_ref[0] = jnp.where(row < HEAD_DIM, kt, kext_ref[0])
    kx_ref[1] = jnp.where(row >= HEAD_DIM, kt, kext_ref[1])

    r_i = lax.broadcasted_iota(jnp.int32, (MOBA_BLOCK, MOBA_BLOCK), 0)
    c_i = lax.broadcasted_iota(jnp.int32, (MOBA_BLOCK, MOBA_BLOCK), 1)
    causal = c_i <= r_i
    lane = lax.broadcasted_iota(jnp.int32, (MOBA_BLOCK, PAIR), 1)
    for j in range(n_blk):
        w = (j + 1) * MOBA_BLOCK
        rows = slice(j * MOBA_BLOCK, w)
        outs = []
        for e in range(2):
            s = jnp.dot(qext_ref[0, e, rows, :], kx_ref[e, :, :w], preferred_element_type=F32)
            own = jnp.where(causal, s[:, w - MOBA_BLOCK:], NEG_BIG)
            s = own if j == 0 else jnp.concatenate([s[:, :w - MOBA_BLOCK], own], axis=1)
            m = jnp.max(s, axis=-1, keepdims=True)
            p = jnp.exp(s - m)
            l = jnp.sum(p, axis=-1, keepdims=True)
            o = jnp.dot(p.astype(BF16), vb_ref[0, :w, :], preferred_element_type=F32)
            outs.append(o / l)
        o_ref[0, rows, :] = jnp.where(lane < HEAD_DIM, outs[0], outs[1]).astype(BF16)


def _back_kernel(x_ref, a_ref, att_ref, ga_ref, gb_ref, wa_ref, wb_ref, wo_ref, gpm_ref, gpf_ref,
                 wg_ref, wu_ref, wd_ref, gpo_ref, o_ref):
    ya = jnp.dot(a_ref[...], wa_ref[...], preferred_element_type=F32)
    yb = jnp.dot(att_ref[...], wb_ref[...], preferred_element_type=F32)
    m = jax.nn.sigmoid(ga_ref[...]) * ya + jax.nn.sigmoid(gb_ref[...]) * yb
    y = jnp.dot(m.astype(BF16), wo_ref[...], preferred_element_type=F32)
    x1 = x_ref[...] + _rms(y, gpm_ref[...])
    h2 = _rms(x1, gpf_ref[...]).astype(BF16)
    acc = jnp.zeros(x1.shape, F32)
    for c in range(D_FF // FF_CHUNK):
        cols = slice(c * FF_CHUNK, (c + 1) * FF_CHUNK)
        g = jnp.dot(h2, wg_ref[:, cols], preferred_element_type=F32)
        up = jnp.dot(h2, wu_ref[:, cols], preferred_element_type=F32)
        act = (g * jax.nn.sigmoid(g)) * up
        acc = acc + jnp.dot(act.astype(BF16), wd_ref[cols, :], preferred_element_type=F32)
    o_ref[...] = x1 + _rms(acc, gpo_ref[...])


def _back(x, a, att, ga, gb, w_a, w_b, w_o, gpm, gpf, w_g, w_u, w_d, gpo):
    n = x.shape[0]
    tm = min(TILE_BACK, n)
    row = lambda w: pl.BlockSpec((tm, w), lambda i: (i, 0))
    return pl.pallas_call(
        _back_kernel,
        grid=(n // tm,),
        in_specs=[row(D_MODEL), row(SGU_WIDTH), row(ATTN_WIDTH), row(D_MODEL), row(D_MODEL),
                  _resident(w_a.shape), _resident(w_b.shape), _resident(w_o.shape),
                  _resident(gpm.shape), _resident(gpf.shape),
                  _resident(w_g.shape), _resident(w_u.shape), _resident(w_d.shape), _resident(gpo.shape)],
        out_specs=row(D_MODEL),
        out_shape=jax.ShapeDtypeStruct((n, D_MODEL), F32),
        compiler_params=pltpu.CompilerParams(dimension_semantics=("arbitrary",),
                                             vmem_limit_bytes=VMEM_LIMIT_BACK),
        name="mixer_back_ffn",
    )(x, a, att, ga, gb, w_a, w_b, w_o, gpm, gpf, w_g, w_u, w_d, gpo)


def _front_sample_kernel(x_ref, gpre_ref, win_ref, gsgu_ref, coef_ref, bias_ref,
                         a_ref, vn_ref, q_ref, k_ref, v_ref, ga_ref, gb_ref):
    h = _rms(x_ref[...], gpre_ref[...]).astype(BF16)

    def proj(lo, hi):
        return jnp.dot(h, win_ref[:, lo:hi], preferred_element_type=F32)

    c0 = SGU_WIDTH
    c1 = 2 * SGU_WIDTH
    c2 = c1 + ATTN_WIDTH
    c3 = c2 + ATTN_WIDTH
    c4 = c3 + ATTN_WIDTH
    c5 = c4 + D_MODEL
    u = proj(0, c0)
    vn = _rms(proj(c0, c1), gsgu_ref[...])
    vn_ref[...] = vn
    a_ref[...] = (u * (vn * coef_ref[...] + bias_ref[...])).astype(BF16)
    q_ref[...] = proj(c1, c2)
    k_ref[...] = proj(c2, c3)
    v_ref[...] = proj(c3, c4)
    ga_ref[...] = proj(c4, c5)
    gb_ref[...] = proj(c5, c5 + D_MODEL)


def _lane_bcast_col(row, n_rows):
    return jnp.broadcast_to(row, (n_rows, row.shape[1])).T


def _head_sums(row):
    return [jnp.sum(row[:, h * HEAD_DIM:(h + 1) * HEAD_DIM], axis=1, keepdims=True) for h in range(N_HEADS)]


def _tree_sum(xs):
    while len(xs) > 1:
        xs = [xs[i] + xs[i + 1] for i in range(0, len(xs) - 1, 2)] + ([xs[-1]] if len(xs) % 2 else [])
    return xs[0]


def _paged_block_scores(ck_rows, row_idx, q_lanes, *, n_samples, n_pages):
    chunks = n_pages * SC_CHUNKS_PER_PAGE
    n_blocks = n_pages // PAGES_PER_BLOCK
    per_worker = n_samples // SC_WORKERS
    idx_len = chunks * SC_SLABS_PER_CHUNK
    q_len = ATTN_WIDTH * SC_LANES
    out_len = n_blocks * N_HEADS * SC_LANES
    heads_per_chunk = N_HEADS // SC_CHUNKS_PER_PAGE
    slabs_per_head = HEAD_DIM // SC_SLAB_ROWS
    mesh = plsc.VectorSubcoreMesh(core_axis_name="c", subcore_axis_name="s")

    @functools.partial(
        pl.kernel, mesh=mesh,
        out_type=jax.ShapeDtypeStruct((n_samples * out_len,), F32),
        scratch_types=[pltpu.VMEM((idx_len,), jnp.int32),
                       pltpu.VMEM((q_len,), F32),
                       pltpu.VMEM((2, SC_SLABS_PER_CHUNK, SC_SLAB_ROWS, PAGE_SIZE), F32),
                       pltpu.VMEM((out_len,), F32),
                       pltpu.SemaphoreType.DMA((2,))],
        name="sample_block_scores_sc",
    )
    def scores(ck_hbm, idx_hbm, q_hbm, out_hbm, idx_v, q_v, rows_v, out_v, sems):
        wid = lax.axis_index("s") * SC_CORES + lax.axis_index("c")

        def fetch(c, slot):
            ids = idx_v.at[pl.ds(pl.multiple_of(c * SC_SLABS_PER_CHUNK, SC_SLABS_PER_CHUNK), SC_SLABS_PER_CHUNK)]
            return pltpu.make_async_copy(ck_hbm.at[ids], rows_v.at[slot], sems.at[slot])

        @pl.loop(0, per_worker)
        def _(t):
            n = wid * per_worker + t
            pltpu.sync_copy(idx_hbm.at[pl.ds(pl.multiple_of(n * idx_len, idx_len), idx_len)], idx_v)
            pltpu.sync_copy(q_hbm.at[pl.ds(pl.multiple_of(n * q_len, q_len), q_len)], q_v)

            @pl.loop(0, out_len, step=SC_LANES)
            def _(o):
                out_v[pl.ds(pl.multiple_of(o, SC_LANES), SC_LANES)] = jnp.zeros((SC_LANES,), F32)

            fetch(0, 0).start()
            fetch(1, 1).start()

            @pl.loop(0, chunks, step=2)
            def _(c0):
                blk = c0 // (SC_CHUNKS_PER_PAGE * PAGES_PER_BLOCK)
                for slot in range(2):
                    fetch(c0 + slot, slot).wait()
                    for hl in range(heads_per_chunk):
                        head = slot * heads_per_chunk + hl
                        ooff = pl.multiple_of((blk * N_HEADS + head) * SC_LANES, SC_LANES)
                        for ii in range(slabs_per_head):
                            slab = hl * slabs_per_head + ii

                            @pl.loop(0, SC_SLAB_ROWS, step=SC_ROW_UNROLL)
                            def _(j0, slab=slab, head=head, ii=ii, ooff=ooff):
                                terms = []
                                for jj in range(SC_ROW_UNROLL):
                                    j = j0 + jj
                                    parts = [rows_v[slot, slab, j, pl.ds(kk * SC_LANES, SC_LANES)]
                                             for kk in range(PAGE_SIZE // SC_LANES)]
                                    qoff = (head * HEAD_DIM + ii * SC_SLAB_ROWS + j) * SC_LANES
                                    terms.append(_tree_sum(parts) * q_v[pl.ds(pl.multiple_of(qoff, SC_LANES), SC_LANES)])
                                plsc.addupdate(out_v.at[pl.ds(ooff, SC_LANES)], _tree_sum(terms))

                    @pl.when(c0 + slot + 2 < chunks)
                    def _():
                        fetch(c0 + slot + 2, slot).start()

            pltpu.sync_copy(out_v, out_hbm.at[pl.ds(pl.multiple_of(n * out_len, out_len), out_len)])

    return scores(ck_rows, row_idx, q_lanes)


def _topk_kernel(g_ref, q_ref, k_ref, idx_ref, *, n_past_blk):
    per_step = g_ref.shape[0]
    base = pl.program_id(0) * per_step
    lane = lax.broadcasted_iota(jnp.int32, (N_HEADS, LANES), 1)
    sub = lax.broadcasted_iota(jnp.int32, (N_HEADS, LANES), 0)
    lane_f = lane.astype(F32)
    for i in range(per_step):
        part = jnp.concatenate([g_ref[i], jnp.zeros((LANES - n_past_blk, LANES), F32)], axis=0)
        gate = jnp.sum(part.T.reshape(N_HEADS, SC_LANES, LANES), axis=1)
        q_row = q_ref[pl.ds(base + i, 1), :]
        own = _head_sums(q_row * k_ref[pl.ds(base + i, 1), :])
        for h in range(N_HEADS):
            gate = jnp.where((lane == n_past_blk) & (sub == h), own[h], gate)
        gate = gate * (1.0 / MOBA_BLOCK)
        gm = jnp.where(lane < n_past_blk, gate, -jnp.inf)
        out = jnp.zeros((N_HEADS, LANES), F32)
        for r in range(MOBA_TOPK):
            mx = jnp.max(gm, axis=1, keepdims=True)
            pick = jnp.min(jnp.where(gm == mx, lane_f, float(LANES)), axis=1, keepdims=True)
            out = jnp.where(lane == r, pick, out)
            gm = jnp.where(lane_f == pick, -jnp.inf, gm)
        idx_ref[i] = out.astype(jnp.int32)


def _decode_attn_kernel(pt_ref, idx_ref, q_ref, k_ref, v_ref, ck_ref, cv_ref, o_ref,
                        kbuf_ref, vbuf_ref, ksem_ref, vsem_ref, *, n_samples, n_pages, past_len):
    n = pl.program_id(0)
    n_tiles = MOBA_TOPK * PAGES_PER_BLOCK
    per_sample = N_HEADS * n_tiles

    def copies(sample, slot):
        out = []
        for h in range(N_HEADS):
            for j in range(MOBA_TOPK):
                blk = idx_ref[sample * (N_HEADS * MOBA_TOPK) + h * MOBA_TOPK + j]
                for pg in range(PAGES_PER_BLOCK):
                    phys = pt_ref[sample * n_pages + blk * PAGES_PER_BLOCK + pg]
                    i = (h * MOBA_TOPK + j) * PAGES_PER_BLOCK + pg
                    out.append(pltpu.make_async_copy(ck_ref.at[phys, h], kbuf_ref.at[slot, i], ksem_ref.at[slot]))
                    out.append(pltpu.make_async_copy(cv_ref.at[phys, h], vbuf_ref.at[slot, i], vsem_ref.at[slot]))
        return out

    @pl.when(n == 0)
    def _():
        for cp in copies(0, 0):
            cp.start()

    @pl.when(n + 1 < n_samples)
    def _():
        for cp in copies(n + 1, (n + 1) % 2):
            cp.start()

    slot = n % 2
    for i in range(per_sample):
        pltpu.make_async_copy(ck_ref.at[0, 0], kbuf_ref.at[slot, i], ksem_ref.at[slot]).wait()
        pltpu.make_async_copy(cv_ref.at[0, 0], vbuf_ref.at[slot, i], vsem_ref.at[slot]).wait()

    q_row = q_ref[pl.ds(n, 1), :]
    k_row = k_ref[pl.ds(n, 1), :]
    v_row = v_ref[pl.ds(n, 1), :]
    qb = _lane_bcast_col(q_row, PAGE_SIZE)
    s_new = _head_sums(q_row * k_row)
    lane_row = lax.broadcasted_iota(jnp.int32, (1, PAGE_SIZE), 1)
    lane_w = lax.broadcasted_iota(jnp.int32, (1, ATTN_WIDTH), 1)
    acc_heads = []
    p_new_row = jnp.zeros((1, ATTN_WIDTH), F32)
    for h, slope in enumerate(_slopes()):
        qh = qb[h * HEAD_DIM:(h + 1) * HEAD_DIM, :]
        scores = []
        for j in range(MOBA_TOPK):
            blk = idx_ref[n * (N_HEADS * MOBA_TOPK) + h * MOBA_TOPK + j]
            for pg in range(PAGES_PER_BLOCK):
                i = (h * MOBA_TOPK + j) * PAGES_PER_BLOCK + pg
                s = jnp.sum(kbuf_ref[slot, i] * qh, axis=0, keepdims=True) * SCALE
                key_pos = blk * MOBA_BLOCK + pg * PAGE_SIZE + lane_row
                scores.append(s - slope * (past_len - key_pos).astype(F32))
        s_n = s_new[h] * SCALE
        m = s_n
        for s in scores:
            m = jnp.maximum(m, jnp.max(s, axis=1, keepdims=True))
        p_n = jnp.exp(s_n - m)
        l = p_n
        acc = jnp.zeros((HEAD_DIM, PAGE_SIZE), F32)
        for t, s in enumerate(scores):
            p = jnp.exp(s - m)
            l = l + jnp.sum(p, axis=1, keepdims=True)
            acc = acc + vbuf_ref[slot, h * n_tiles + t] * p
        acc_heads.append(acc / l)
        p_new_row = jnp.where(lane_w // HEAD_DIM == h, p_n / l, p_new_row)
    acc_all = jnp.concatenate(acc_heads, axis=0)
    o_row = jnp.sum(acc_all.T, axis=0, keepdims=True) + p_new_row * v_row
    o_ref[0] = o_row


def kernel(x_prompt, x_sample, cache_k, cache_v, page_table, g_pre_mix, w_in, g_sgu, w_spatial, b_spatial,
           w_a, w_b, w_o, g_post_mix, g_pre_ffn, w_gate, w_up, w_down, g_post_ffn):
    depth = w_in.shape[0]
    assert depth == 1, "kernels are written for a single layer"
    bsz, seq, _ = x_prompt.shape
    n_samples, dec_seq, _ = x_sample.shape
    assert dec_seq == 1
    n_pool = cache_k.shape[1]
    n_pages = page_table.shape[1]
    past_len = n_pages * PAGE_SIZE
    n_past_blk = past_len // MOBA_BLOCK
    n_blk = seq // MOBA_BLOCK
    assert n_blk * N_HEADS <= HEAD_DIM and n_past_blk < LANES
    assert n_samples % SC_WORKERS == 0 and n_samples % TOPK_PER_STEP == 0 and n_pages % PAGES_PER_BLOCK == 0

    l = 0
    w_in_b = w_in[l].astype(BF16)
    back_w = (w_a[l].astype(BF16), w_b[l].astype(BF16), w_o[l].astype(BF16), g_post_mix[l][None], g_pre_ffn[l][None],
              w_gate[l].astype(BF16), w_up[l].astype(BF16), w_down[l].astype(BF16), g_post_ffn[l][None])
    gpre = g_pre_mix[l][None]
    gsgu = g_sgu[l][None]
    in_cols = w_in_b.shape[1]

    xs = x_sample.reshape(n_samples, D_MODEL)
    coef = jnp.repeat(w_spatial[l, :, 0, 0], CHUNK)[None]
    bias = jnp.repeat(b_spatial[l, :, 0], CHUNK)[None]
    full = lambda shape: pl.BlockSpec(shape, lambda *_: (0,) * len(shape))
    s_w = lambda w, dt: jax.ShapeDtypeStruct((n_samples, w), dt)
    a_s, vn_s, q_s, k_s, v_s, ga_s, gb_s = pl.pallas_call(
        _front_sample_kernel,
        grid=(1,),
        in_specs=[full((n_samples, D_MODEL)), full((1, D_MODEL)), full((D_MODEL, in_cols)), full((1, SGU_WIDTH)),
                  full((1, SGU_WIDTH)), full((1, SGU_WIDTH))],
        out_specs=[full((n_samples, SGU_WIDTH)), full((n_samples, SGU_WIDTH)), full((n_samples, ATTN_WIDTH)),
                   full((n_samples, ATTN_WIDTH)), full((n_samples, ATTN_WIDTH)),
                   full((n_samples, D_MODEL)), full((n_samples, D_MODEL))],
        out_shape=[s_w(SGU_WIDTH, BF16), s_w(SGU_WIDTH, F32), s_w(ATTN_WIDTH, F32), s_w(ATTN_WIDTH, F32),
                   s_w(ATTN_WIDTH, F32), s_w(D_MODEL, F32), s_w(D_MODEL, F32)],
        compiler_params=pltpu.CompilerParams(dimension_semantics=("arbitrary",),
                                             vmem_limit_bytes=VMEM_LIMIT_FRONT),
        name="sample_front",
    )(xs, gpre, w_in_b, gsgu, coef, bias)

    ck_t = jnp.transpose(cache_k[l], (0, 2, 3, 1))
    cv_t = jnp.transpose(cache_v[l], (0, 2, 3, 1))
    pt_flat = page_table.reshape(-1)
    slabs_per_page = ATTN_WIDTH // SC_SLAB_ROWS
    ck_rows = ck_t.reshape(n_pool * slabs_per_page, SC_SLAB_ROWS, PAGE_SIZE)
    row_idx = (page_table[:, :, None] * slabs_per_page + jnp.arange(slabs_per_page, dtype=jnp.int32)).reshape(-1)
    q_lanes = jnp.broadcast_to(q_s[:, :, None], (n_samples, ATTN_WIDTH, SC_LANES)).reshape(-1)
    gate_parts = _paged_block_scores(ck_rows, row_idx, q_lanes, n_samples=n_samples, n_pages=n_pages)
    gate_parts = gate_parts.reshape(n_samples, n_past_blk, N_HEADS * SC_LANES)

    n_tiles = seq // TILE_FRONT
    tile3 = lambda w: pl.BlockSpec((1, TILE_FRONT, w), lambda b, j: (b, j, 0))
    tile3t = lambda w: pl.BlockSpec((1, w, TILE_FRONT), lambda b, j: (b, 0, j))
    a_p, qext, kt, vt, vb, ga, gb = pl.pallas_call(
        _front_kernel,
        grid=(bsz, n_tiles),
        in_specs=[tile3(D_MODEL), _resident((1, D_MODEL)), _resident((D_MODEL, in_cols)), _resident((1, SGU_WIDTH)),
                  _resident((SGU_GROUPS, CHUNK, CHUNK)), _resident((CHUNK, SGU_GROUPS))],
        out_specs=[tile3(SGU_WIDTH),
                   pl.BlockSpec((1, N_HEADS, TILE_FRONT, PAIR), lambda b, j: (b, 0, j, 0)),
                   tile3t(ATTN_WIDTH), tile3t(ATTN_WIDTH), tile3(ATTN_WIDTH), tile3(D_MODEL), tile3(D_MODEL)],
        out_shape=[jax.ShapeDtypeStruct((bsz, seq, SGU_WIDTH), BF16),
                   jax.ShapeDtypeStruct((bsz, N_HEADS, seq, PAIR), BF16),
                   jax.ShapeDtypeStruct((bsz, ATTN_WIDTH, seq), F32),
                   jax.ShapeDtypeStruct((bsz, ATTN_WIDTH, seq), F32),
                   jax.ShapeDtypeStruct((bsz, seq, ATTN_WIDTH), BF16),
                   jax.ShapeDtypeStruct((bsz, seq, D_MODEL), F32),
                   jax.ShapeDtypeStruct((bsz, seq, D_MODEL), F32)],
        scratch_shapes=[pltpu.VMEM((n_blk, ATTN_WIDTH), F32)],
        compiler_params=pltpu.CompilerParams(dimension_semantics=("arbitrary", "arbitrary"),
                                             vmem_limit_bytes=VMEM_LIMIT_FRONT),
        name="prompt_front",
    )(x_prompt, gpre, w_in_b, gsgu, w_spatial[l], b_spatial[l].T)

    kext = _key_ext_const(seq)
    att_p = pl.pallas_call(
        _attn_kernel,
        grid=(bsz, N_PAIRS),
        in_specs=[pl.BlockSpec((1, 2, seq, PAIR), lambda b, p: (b, p, 0, 0)),
                  pl.BlockSpec((1, PAIR, seq), lambda b, p: (b, p, 0)),
                  pl.BlockSpec((1, seq, PAIR), lambda b, p: (b, 0, p)),
                  pl.BlockSpec((2, PAIR, seq), lambda b, p: (p, 0, 0))],
        out_specs=pl.BlockSpec((1, seq, PAIR), lambda b, p: (b, 0, p)),
        out_shape=jax.ShapeDtypeStruct((bsz, seq, ATTN_WIDTH), BF16),
        scratch_shapes=[pltpu.VMEM((2, PAIR, seq), BF16)],
        compiler_params=pltpu.CompilerParams(dimension_semantics=("arbitrary", "arbitrary"),
                                             vmem_limit_bytes=VMEM_LIMIT_ATTN),
        name="prompt_moba",
    )(qext, kt, vb, kext)

    n_tok = bsz * seq
    flat = lambda z: z.reshape(n_tok, z.shape[-1])
    y_prompt = _back(flat(x_prompt), flat(a_p), flat(att_p), flat(ga), flat(gb), *back_w).reshape(bsz, seq, D_MODEL)

    idx = pl.pallas_call(
        functools.partial(_topk_kernel, n_past_blk=n_past_blk),
        grid=(n_samples // TOPK_PER_STEP,),
        in_specs=[pl.BlockSpec((TOPK_PER_STEP, n_past_blk, N_HEADS * SC_LANES), lambda i: (i, 0, 0)),
                  full((n_samples, ATTN_WIDTH)), full((n_samples, ATTN_WIDTH))],
        out_specs=pl.BlockSpec((TOPK_PER_STEP, N_HEADS, LANES), lambda i: (i, 0, 0)),
        out_shape=jax.ShapeDtypeStruct((n_samples, N_HEADS, LANES), jnp.int32),
        compiler_params=pltpu.CompilerParams(dimension_semantics=("arbitrary",)),
        name="sample_block_topk",
    )(gate_parts, q_s, k_s)
    idx_flat = idx[:, :, :MOBA_TOPK].reshape(-1)
    n_gather = N_HEADS * MOBA_TOPK * PAGES_PER_BLOCK
    att_s = pl.pallas_call(
        functools.partial(_decode_attn_kernel, n_samples=n_samples, n_pages=n_pages, past_len=past_len),
        grid_spec=pltpu.PrefetchScalarGridSpec(
            num_scalar_prefetch=2,
            grid=(n_samples,),
            in_specs=[pl.BlockSpec((n_samples, ATTN_WIDTH), lambda i, pt, ix: (0, 0)),
                      pl.BlockSpec((n_samples, ATTN_WIDTH), lambda i, pt, ix: (0, 0)),
                      pl.BlockSpec((n_samples, ATTN_WIDTH), lambda i, pt, ix: (0, 0)),
                      pl.BlockSpec(memory_space=pl.ANY), pl.BlockSpec(memory_space=pl.ANY)],
            out_specs=pl.BlockSpec((1, 1, ATTN_WIDTH), lambda i, pt, ix: (i, 0, 0)),
            scratch_shapes=[pltpu.VMEM((2, n_gather, HEAD_DIM, PAGE_SIZE), F32),
                            pltpu.VMEM((2, n_gather, HEAD_DIM, PAGE_SIZE), F32),
                            pltpu.SemaphoreType.DMA((2,)), pltpu.SemaphoreType.DMA((2,))]),
        out_shape=jax.ShapeDtypeStruct((n_samples, 1, ATTN_WIDTH), F32),
        compiler_params=pltpu.CompilerParams(dimension_semantics=("arbitrary",),
                                             vmem_limit_bytes=VMEM_LIMIT_DECODE),
        name="sample_moba",
    )(pt_flat, idx_flat, q_s, k_s, v_s, ck_t, cv_t)
    att_s = att_s.reshape(n_samples, ATTN_WIDTH).astype(BF16)

    y_sample = _back(xs, a_s, att_s, ga_s, gb_s, *back_w).reshape(n_samples, 1, D_MODEL)

    heads_t = lambda zt: jnp.transpose(zt.reshape(1, bsz, N_HEADS, HEAD_DIM, seq), (0, 1, 4, 2, 3))
    heads_s = lambda z: z.reshape(1, n_samples, 1, N_HEADS, HEAD_DIM)
    return (y_prompt, y_sample, heads_t(kt), heads_t(vt), heads_s(k_s), heads_s(v_s),
            vn_s.reshape(1, n_samples, 1, SGU_WIDTH))
```

```python
import functools

import numpy as np
import jax
import jax.numpy as jnp
from jax import lax
from jax.experimental import pallas as pl
from jax.experimental.pallas import tpu as pltpu
from jax.experimental.pallas import tpu_sc as plsc

F32 = jnp.float32
BF16 = jnp.bfloat16

D_MODEL = 1024
N_HEADS = 8
HEAD_DIM = 64
ATTN_WIDTH = N_HEADS * HEAD_DIM
MOBA_BLOCK = 256
MOBA_TOPK = 3
CHUNK = 128
SGU_GROUPS = 4
SGU_WIDTH = 512
D_FF = 2816
PAGE_SIZE = 128
EPS = 1e-6
ALIBI_MAX_BIAS = 8.0

LANES = 128
PAIR = 2 * HEAD_DIM
N_PAIRS = N_HEADS // 2
NEG_BIG = -1e30
PAGES_PER_BLOCK = MOBA_BLOCK // PAGE_SIZE
SCALE = HEAD_DIM ** -0.5

TILE_FRONT = MOBA_BLOCK
TILE_BACK = 512
FF_CHUNK = 256
TOPK_PER_STEP = 16

SC_CORES = 2
SC_SUBCORES = 16
SC_WORKERS = SC_CORES * SC_SUBCORES
SC_LANES = 16
SC_SLAB_ROWS = 16
SC_CHUNKS_PER_PAGE = 4
SC_SLABS_PER_CHUNK = ATTN_WIDTH // SC_CHUNKS_PER_PAGE // SC_SLAB_ROWS
SC_ROW_UNROLL = 8

VMEM_LIMIT_FRONT = 44 * 1024 * 1024
VMEM_LIMIT_ATTN = 48 * 1024 * 1024
VMEM_LIMIT_BACK = 56 * 1024 * 1024
VMEM_LIMIT_DECODE = 32 * 1024 * 1024


def _slopes():
    return [2.0 ** (-(ALIBI_MAX_BIAS / N_HEADS) * (h + 1)) for h in range(N_HEADS)]


def _rms(x, g):
    return x * lax.rsqrt(jnp.mean(x * x, axis=-1, keepdims=True) + EPS) * g


def _split_bf16(x):
    hi = x.astype(BF16)
    lo = (x - hi.astype(F32)).astype(BF16)
    return hi, lo


def _dot_nt_f32(a, b):
    dn = (((1,), (1,)), ((), ()))
    ah, al = _split_bf16(a)
    bh, bl = _split_bf16(b)
    d = functools.partial(lax.dot_general, dimension_numbers=dn, preferred_element_type=F32)
    return d(ah, bh) + d(ah, bl) + d(al, bh)


def _resident(shape):
    nd = len(shape)
    return pl.BlockSpec(shape, lambda *_: (0,) * nd, pipeline_mode=pl.Buffered(1))


def _front_kernel(x_ref, gpre_ref, win_ref, gsgu_ref, wsp_ref, bspt_ref,
                  a_ref, qext_ref, kt_ref, vt_ref, vb_ref, ga_ref, gb_ref, ksum_ref):
    j = pl.program_id(1)

    @pl.when(j == 0)
    def _():
        ksum_ref[...] = jnp.zeros_like(ksum_ref)

    h = _rms(x_ref[0], gpre_ref[...]).astype(BF16)

    def proj(lo, hi):
        return jnp.dot(h, win_ref[:, lo:hi], preferred_element_type=F32)

    c0 = SGU_WIDTH
    c1 = 2 * SGU_WIDTH
    c2 = c1 + ATTN_WIDTH
    c3 = c2 + ATTN_WIDTH
    c4 = c3 + ATTN_WIDTH
    c5 = c4 + D_MODEL
    u = proj(0, c0)
    v = proj(c0, c1)
    q = proj(c1, c2)
    k = proj(c2, c3)
    vv = proj(c3, c4)
    ga_ref[0] = proj(c4, c5)
    gb_ref[0] = proj(c5, c5 + D_MODEL)

    vn = _rms(v, gsgu_ref[...]).astype(BF16)
    tri_r = lax.broadcasted_iota(jnp.int32, (CHUNK, CHUNK), 0)
    tri_c = lax.broadcasted_iota(jnp.int32, (CHUNK, CHUNK), 1)
    for g in range(SGU_GROUPS):
        wg = jnp.where(tri_c <= tri_r, wsp_ref[g], 0.0).astype(BF16)
        cols = slice(g * CHUNK, (g + 1) * CHUNK)
        for c in range(TILE_FRONT // CHUNK):
            rows = slice(c * CHUNK, (c + 1) * CHUNK)
            sv = jnp.dot(wg, vn[rows, cols], preferred_element_type=F32) + bspt_ref[:, g:g + 1]
            a_ref[0, rows, cols] = (u[rows, cols] * sv).astype(BF16)

    kt_ref[0] = k.T
    vt_ref[0] = vv.T
    vb_ref[0] = vv.astype(BF16)

    ksum_ref[pl.ds(j, 1), :] = jnp.sum(k, axis=0, keepdims=True)
    km = ksum_ref[...] * (1.0 / MOBA_BLOCK)
    n_blk = km.shape[0]
    km_rep = jnp.concatenate([km] * N_HEADS + [jnp.zeros((LANES - N_HEADS * n_blk, ATTN_WIDTH), F32)], axis=0)
    r_i = lax.broadcasted_iota(jnp.int32, (LANES, ATTN_WIDTH), 0)
    c_i = lax.broadcasted_iota(jnp.int32, (LANES, ATTN_WIDTH), 1)
    head_rows = (r_i // n_blk == c_i // HEAD_DIM) & (r_i < N_HEADS * n_blk)
    km_bd = jnp.where(head_rows, km_rep, 0.0)
    gate_t = _dot_nt_f32(km_bd, q)

    blk_i = lax.broadcasted_iota(jnp.int32, (n_blk, TILE_FRONT), 0)
    cand = blk_i < j
    bias_rows = []
    for hh in range(N_HEADS):
        gm = jnp.where(cand, gate_t[hh * n_blk:(hh + 1) * n_blk, :], -jnp.inf)
        rank = jnp.zeros((n_blk, TILE_FRONT), F32)
        for m in range(n_blk):
            gmm = gm[m:m + 1, :]
            ahead = (gmm > gm) | ((gmm == gm) & (blk_i > m))
            rank = rank + jnp.where(ahead, 1.0, 0.0)
        keep = (cand & (rank < float(MOBA_TOPK))) | (blk_i == j)
        bias_rows.append(jnp.where(keep, 0.0, NEG_BIG))
    bias_rows.append(jnp.full((LANES - N_HEADS * n_blk, TILE_FRONT), NEG_BIG, F32))
    sel = jnp.concatenate(bias_rows, axis=0).T
    sel_sw = pltpu.roll(sel, LANES // 2, axis=1)

    lane = lax.broadcasted_iota(jnp.int32, (TILE_FRONT, LANES), 1)
    for hh in range(N_HEADS):
        p = hh // 2
        qp = q[:, p * PAIR:(p + 1) * PAIR] * SCALE
        if hh % 2 == 0:
            lo = HEAD_DIM + hh * n_blk
            ext = jnp.where((lane >= lo) & (lane < lo + n_blk), sel_sw,
                            jnp.where(lane >= LANES - 2, 1.0, 0.0))
            qx = jnp.where(lane < HEAD_DIM, qp, ext)
        else:
            lo = hh * n_blk
            ext = jnp.where((lane >= lo) & (lane < lo + n_blk), sel,
                            jnp.where(lane < 2, 1.0, 0.0))
            qx = jnp.where(lane >= HEAD_DIM, qp, ext)
        qext_ref[0, hh] = qx.astype(BF16)


def _key_ext_const(seq):
    n_blk = seq // MOBA_BLOCK
    pos = np.arange(seq)
    off = (pos % MOBA_BLOCK).astype(np.float32)
    blk = pos // MOBA_BLOCK
    out = np.zeros((N_HEADS, PAIR, seq), np.float32)
    for hh, slope in enumerate(_slopes()):
        if hh % 2 == 0:
            base, r_off, r_blk = HEAD_DIM + hh * n_blk, PAIR - 2, PAIR - 1
        else:
            base, r_off, r_blk = hh * n_blk, 0, 1
        for n in range(n_blk):
            out[hh, base + n] = (blk == n)
        out[hh, r_off] = slope * off
        out[hh, r_blk] = slope * MOBA_BLOCK * blk
    return jnp.asarray(out, dtype=BF16)


def _attn_kernel(qext_ref, kt_ref, vb_ref, kext_ref, o_ref, kx_ref):
    seq = kt_ref.shape[2]
    n_blk = seq // MOBA_BLOCK
    kt = kt_ref[0].astype(BF16)
    row = lax.broadcasted_iota(jnp.int32, (PAIR, seq), 0)
    kx_ref[0] = jnp.where(row < HEAD_DIM, kt, kext_ref[0])
    kx_ref[1] = jnp.where(row >= HEAD_DIM, kt, kext_ref[1])

    r_i = lax.broadcasted_iota(jnp.int32, (MOBA_BLOCK, MOBA_BLOCK), 0)
    c_i = lax.broadcasted_iota(jnp.int32, (MOBA_BLOCK, MOBA_BLOCK), 1)
    causal = c_i <= r_i
    lane = lax.broadcasted_iota(jnp.int32, (MOBA_BLOCK, PAIR), 1)
    for j in range(n_blk):
        w = (j + 1) * MOBA_BLOCK
        rows = slice(j * MOBA_BLOCK, w)
        outs = []
        for e in range(2):
            s = jnp.dot(qext_ref[0, e, rows, :], kx_ref[e, :, :w], preferred_element_type=F32)
            own = jnp.where(causal, s[:, w - MOBA_BLOCK:], NEG_BIG)
            s = own if j == 0 else jnp.concatenate([s[:, :w - MOBA_BLOCK], own], axis=1)
            m = jnp.max(s, axis=-1, keepdims=True)
            p = jnp.exp(s - m)
            l = jnp.sum(p, axis=-1, keepdims=True)
            o = jnp.dot(p.astype(BF16), vb_ref[0, :w, :], preferred_element_type=F32)
            outs.append(o / l)
        o_ref[0, rows, :] = jnp.where(lane < HEAD_DIM, outs[0], outs[1]).astype(BF16)


def _back_kernel(x_ref, a_ref, att_ref, ga_ref, gb_ref, wa_ref, wb_ref, wo_ref, gpm_ref, gpf_ref,
                 wg_ref, wu_ref, wd_ref, gpo_ref, o_ref):
    ya = jnp.dot(a_ref[...], wa_ref[...], preferred_element_type=F32)
    yb = jnp.dot(att_ref[...], wb_ref[...], preferred_element_type=F32)
    m = jax.nn.sigmoid(ga_ref[...]) * ya + jax.nn.sigmoid(gb_ref[...]) * yb
    y = jnp.dot(m.astype(BF16), wo_ref[...], preferred_element_type=F32)
    x1 = x_ref[...] + _rms(y, gpm_ref[...])
    h2 = _rms(x1, gpf_ref[...]).astype(BF16)
    acc = jnp.zeros(x1.shape, F32)
    for c in range(D_FF // FF_CHUNK):
        cols = slice(c * FF_CHUNK, (c + 1) * FF_CHUNK)
        g = jnp.dot(h2, wg_ref[:, cols], preferred_element_type=F32)
        up = jnp.dot(h2, wu_ref[:, cols], preferred_element_type=F32)
        act = (g * jax.nn.sigmoid(g)) * up
        acc = acc + jnp.dot(act.astype(BF16), wd_ref[cols, :], preferred_element_type=F32)
    o_ref[...] = x1 + _rms(acc, gpo_ref[...])


def _back(x, a, att, ga, gb, w_a, w_b, w_o, gpm, gpf, w_g, w_u, w_d, gpo):
    n = x.shape[0]
    tm = min(TILE_BACK, n)
    row = lambda w: pl.BlockSpec((tm, w), lambda i: (i, 0))
    return pl.pallas_call(
        _back_kernel,
        grid=(n // tm,),
        in_specs=[row(D_MODEL), row(SGU_WIDTH), row(ATTN_WIDTH), row(D_MODEL), row(D_MODEL),
                  _resident(w_a.shape), _resident(w_b.shape), _resident(w_o.shape),
                  _resident(gpm.shape), _resident(gpf.shape),
                  _resident(w_g.shape), _resident(w_u.shape), _resident(w_d.shape), _resident(gpo.shape)],
        out_specs=row(D_MODEL),
        out_shape=jax.ShapeDtypeStruct((n, D_MODEL), F32),
        compiler_params=pltpu.CompilerParams(dimension_semantics=("arbitrary",),
                                             vmem_limit_bytes=VMEM_LIMIT_BACK),
        name="mixer_back_ffn",
    )(x, a, att, ga, gb, w_a, w_b, w_o, gpm, gpf, w_g, w_u, w_d, gpo)


def _front_sample_kernel(x_ref, gpre_ref, win_ref, gsgu_ref, coef_ref, bias_ref,
                         a_ref, vn_ref, q_ref, k_ref, v_ref, ga_ref, gb_ref):
    h = _rms(x_ref[...], gpre_ref[...]).astype(BF16)

    def proj(lo, hi):
        return jnp.dot(h, win_ref[:, lo:hi], preferred_element_type=F32)

    c0 = SGU_WIDTH
    c1 = 2 * SGU_WIDTH
    c2 = c1 + ATTN_WIDTH
    c3 = c2 + ATTN_WIDTH
    c4 = c3 + ATTN_WIDTH
    c5 = c4 + D_MODEL
    u = proj(0, c0)
    vn = _rms(proj(c0, c1), gsgu_ref[...])
    vn_ref[...] = vn
    a_ref[...] = (u * (vn * coef_ref[...] + bias_ref[...])).astype(BF16)
    q_ref[...] = proj(c1, c2)
    k_ref[...] = proj(c2, c3)
    v_ref[...] = proj(c3, c4)
    ga_ref[...] = proj(c4, c5)
    gb_ref[...] = proj(c5, c5 + D_MODEL)


def _lane_bcast_col(row, n_rows):
    return jnp.broadcast_to(row, (n_rows, row.shape[1])).T


def _head_sums(row):
    return [jnp.sum(row[:, h * HEAD_DIM:(h + 1) * HEAD_DIM], axis=1, keepdims=True) for h in range(N_HEADS)]


def _tree_sum(xs):
    while len(xs) > 1:
        xs = [xs[i] + xs[i + 1] for i in range(0, len(xs) - 1, 2)] + ([xs[-1]] if len(xs) % 2 else [])
    return xs[0]


def _paged_block_scores(ck_rows, row_idx, q_lanes, *, n_samples, n_pages):
    chunks = n_pages * SC_CHUNKS_PER_PAGE
    n_blocks = n_pages // PAGES_PER_BLOCK
    per_worker = n_samples // SC_WORKERS
    idx_len = chunks * SC_SLABS_PER_CHUNK
    q_len = ATTN_WIDTH * SC_LANES
    out_len = n_blocks * N_HEADS * SC_LANES
    rows_per_chunk = SC_SLABS_PER_CHUNK * SC_SLAB_ROWS
    assert SC_SLAB_ROWS % SC_ROW_UNROLL == 0 and HEAD_DIM % SC_ROW_UNROLL == 0
    ring = SC_CHUNKS_PER_PAGE
    mesh = plsc.VectorSubcoreMesh(core_axis_name="c", subcore_axis_name="s")

    @functools.partial(
        pl.kernel, mesh=mesh,
        out_type=jax.ShapeDtypeStruct((n_samples * out_len,), F32),
        scratch_types=[pltpu.VMEM((idx_len,), jnp.int32),
                       pltpu.VMEM((q_len,), F32),
                       pltpu.VMEM((ring, SC_SLABS_PER_CHUNK, SC_SLAB_ROWS, PAGE_SIZE), F32),
                       pltpu.VMEM((out_len,), F32),
                       pltpu.SemaphoreType.DMA((ring,))],
        name="sample_block_scores_sc",
    )
    def scores(ck_hbm, idx_hbm, q_hbm, out_hbm, idx_v, q_v, rows_v, out_v, sems):
        wid = lax.axis_index("s") * SC_CORES + lax.axis_index("c")

        def fetch(c, slot):
            ids = idx_v.at[pl.ds(pl.multiple_of(c * SC_SLABS_PER_CHUNK, SC_SLABS_PER_CHUNK), SC_SLABS_PER_CHUNK)]
            return pltpu.make_async_copy(ck_hbm.at[ids], rows_v.at[slot], sems.at[slot])

        @pl.loop(0, per_worker)
        def _(t):
            n = wid * per_worker + t
            pltpu.sync_copy(idx_hbm.at[pl.ds(pl.multiple_of(n * idx_len, idx_len), idx_len)], idx_v)
            pltpu.sync_copy(q_hbm.at[pl.ds(pl.multiple_of(n * q_len, q_len), q_len)], q_v)

            @pl.loop(0, out_len, step=SC_LANES)
            def _(o):
                out_v[pl.ds(pl.multiple_of(o, SC_LANES), SC_LANES)] = jnp.zeros((SC_LANES,), F32)

            for slot in range(ring):
                fetch(slot, slot).start()

            @pl.loop(0, chunks, step=ring)
            def _(c0):
                blk = c0 // (SC_CHUNKS_PER_PAGE * PAGES_PER_BLOCK)
                for slot in range(ring):
                    fetch(c0 + slot, slot).wait()

                    @pl.loop(0, rows_per_chunk, step=SC_ROW_UNROLL)
                    def _(r0, slot=slot):
                        row0 = slot * rows_per_chunk + r0
                        ooff = pl.multiple_of((blk * N_HEADS + row0 // HEAD_DIM) * SC_LANES, SC_LANES)
                        slab = r0 // SC_SLAB_ROWS
                        j0 = r0 % SC_SLAB_ROWS
                        terms = []
                        for jj in range(SC_ROW_UNROLL):
                            parts = [rows_v[slot, slab, j0 + jj, pl.ds(kk * SC_LANES, SC_LANES)]
                                     for kk in range(PAGE_SIZE // SC_LANES)]
                            qoff = pl.multiple_of((row0 + jj) * SC_LANES, SC_LANES)
                            terms.append(_tree_sum(parts) * q_v[pl.ds(qoff, SC_LANES)])
                        plsc.addupdate(out_v.at[pl.ds(ooff, SC_LANES)], _tree_sum(terms))

                    @pl.when(c0 + slot + ring < chunks)
                    def _():
                        fetch(c0 + slot + ring, slot).start()

            pltpu.sync_copy(out_v, out_hbm.at[pl.ds(pl.multiple_of(n * out_len, out_len), out_len)])

    return scores(ck_rows, row_idx, q_lanes)


def _split3_bf16(x):
    hi = x.astype(BF16)
    r1 = x - hi.astype(F32)
    mid = r1.astype(BF16)
    lo = (r1 - mid.astype(F32)).astype(BF16)
    return hi, mid, lo


def _dot_exact_onehot(x, onehot):
    return sum(jnp.dot(t, onehot, preferred_element_type=F32) for t in _split3_bf16(x))


def _topk_kernel(g_ref, q_ref, k_ref, idx_ref, *, n_past_blk):
    per_step = g_ref.shape[0]
    r_i = lax.broadcasted_iota(jnp.int32, (LANES, LANES), 0)
    c_i = lax.broadcasted_iota(jnp.int32, (LANES, LANES), 1)
    lane_sum = jnp.where(r_i // SC_LANES == c_i, 1.0, 0.0).astype(BF16)
    r_w = lax.broadcasted_iota(jnp.int32, (ATTN_WIDTH, LANES), 0)
    c_w = lax.broadcasted_iota(jnp.int32, (ATTN_WIDTH, LANES), 1)
    head_sum = jnp.where(r_w // HEAD_DIM == c_w, 1.0, 0.0).astype(BF16)
    parts = g_ref[...].reshape(per_step * n_past_blk, N_HEADS * SC_LANES)
    gates = _dot_exact_onehot(parts, lane_sum)
    own = _dot_exact_onehot(q_ref[...] * k_ref[...], head_sum)
    pad = 8
    blk_i = lax.broadcasted_iota(jnp.int32, (n_past_blk + pad, LANES), 0)
    blk_f = blk_i.astype(F32)
    out_row = lax.broadcasted_iota(jnp.int32, (pad, LANES), 0)
    for i in range(per_step):
        gate = jnp.concatenate([gates[i * n_past_blk:(i + 1) * n_past_blk],
                                jnp.broadcast_to(own[i:i + 1], (pad, LANES))], axis=0) * (1.0 / MOBA_BLOCK)
        gm = jnp.where(blk_i < n_past_blk, gate, -jnp.inf)
        out = jnp.zeros((pad, LANES), F32)
        for r in range(MOBA_TOPK):
            mx = jnp.max(gm, axis=0, keepdims=True)
            pick = jnp.min(jnp.where(gm == mx, blk_f, float(LANES)), axis=0, keepdims=True)
            out = jnp.where(out_row == r, pick, out)
            gm = jnp.where(blk_f == pick, -jnp.inf, gm)
        idx_ref[i] = out.astype(jnp.int32)


def _decode_attn_kernel(pt_ref, idx_ref, q_ref, k_ref, v_ref, ck_ref, cv_ref, o_ref,
                        kbuf_ref, vbuf_ref, ksem_ref, vsem_ref, *, n_samples, n_pages, past_len):
    n = pl.program_id(0)
    n_tiles = MOBA_TOPK * PAGES_PER_BLOCK
    per_sample = N_HEADS * n_tiles

    def copies(sample, slot):
        out = []
        for h in range(N_HEADS):
            for j in range(MOBA_TOPK):
                blk = idx_ref[sample * (N_HEADS * MOBA_TOPK) + h * MOBA_TOPK + j]
                for pg in range(PAGES_PER_BLOCK):
                    phys = pt_ref[sample * n_pages + blk * PAGES_PER_BLOCK + pg]
                    i = (h * MOBA_TOPK + j) * PAGES_PER_BLOCK + pg
                    out.append(pltpu.make_async_copy(ck_ref.at[phys, h], kbuf_ref.at[slot, i], ksem_ref.at[slot]))
                    out.append(pltpu.make_async_copy(cv_ref.at[phys, h], vbuf_ref.at[slot, i], vsem_ref.at[slot]))
        return out

    @pl.when(n == 0)
    def _():
        for cp in copies(0, 0):
            cp.start()

    @pl.when(n + 1 < n_samples)
    def _():
        for cp in copies(n + 1, (n + 1) % 2):
            cp.start()

    slot = n % 2
    for i in range(per_sample):
        pltpu.make_async_copy(ck_ref.at[0, 0], kbuf_ref.at[slot, i], ksem_ref.at[slot]).wait()
        pltpu.make_async_copy(cv_ref.at[0, 0], vbuf_ref.at[slot, i], vsem_ref.at[slot]).wait()

    q_row = q_ref[pl.ds(n, 1), :]
    k_row = k_ref[pl.ds(n, 1), :]
    v_row = v_ref[pl.ds(n, 1), :]
    qb = _lane_bcast_col(q_row, PAGE_SIZE)
    s_new = _head_sums(q_row * k_row)
    lane_row = lax.broadcasted_iota(jnp.int32, (1, PAGE_SIZE), 1)
    lane_w = lax.broadcasted_iota(jnp.int32, (1, ATTN_WIDTH), 1)
    acc_heads = []
    p_new_row = jnp.zeros((1, ATTN_WIDTH), F32)
    for h, slope in enumerate(_slopes()):
        qh = qb[h * HEAD_DIM:(h + 1) * HEAD_DIM, :]
        scores = []
        for j in range(MOBA_TOPK):
            blk = idx_ref[n * (N_HEADS * MOBA_TOPK) + h * MOBA_TOPK + j]
            for pg in range(PAGES_PER_BLOCK):
                i = (h * MOBA_TOPK + j) * PAGES_PER_BLOCK + pg
                s = jnp.sum(kbuf_ref[slot, i] * qh, axis=0, keepdims=True) * SCALE
                key_pos = blk * MOBA_BLOCK + pg * PAGE_SIZE + lane_row
                scores.append(s - slope * (past_len - key_pos).astype(F32))
        s_n = s_new[h] * SCALE
        m = s_n
        for s in scores:
            m = jnp.maximum(m, jnp.max(s, axis=1, keepdims=True))
        p_n = jnp.exp(s_n - m)
        l = p_n
        acc = jnp.zeros((HEAD_DIM, PAGE_SIZE), F32)
        for t, s in enumerate(scores):
            p = jnp.exp(s - m)
            l = l + jnp.sum(p, axis=1, keepdims=True)
            acc = acc + vbuf_ref[slot, h * n_tiles + t] * p
        acc_heads.append(acc / l)
        p_new_row = jnp.where(lane_w // HEAD_DIM == h, p_n / l, p_new_row)
    acc_all = jnp.concatenate(acc_heads, axis=0)
    o_row = jnp.sum(acc_all.T, axis=0, keepdims=True) + p_new_row * v_row
    o_ref[0] = o_row


def kernel(x_prompt, x_sample, cache_k, cache_v, page_table, g_pre_mix, w_in, g_sgu, w_spatial, b_spatial,
           w_a, w_b, w_o, g_post_mix, g_pre_ffn, w_gate, w_up, w_down, g_post_ffn):
    depth = w_in.shape[0]
    assert depth == 1, "kernels are written for a single layer"
    bsz, seq, _ = x_prompt.shape
    n_samples, dec_seq, _ = x_sample.shape
    assert dec_seq == 1
    n_pool = cache_k.shape[1]
    n_pages = page_table.shape[1]
    past_len = n_pages * PAGE_SIZE
    n_past_blk = past_len // MOBA_BLOCK
    n_blk = seq // MOBA_BLOCK
    assert n_blk * N_HEADS <= HEAD_DIM and n_past_blk < LANES
    assert n_samples % SC_WORKERS == 0 and n_samples % TOPK_PER_STEP == 0 and n_pages % PAGES_PER_BLOCK == 0

    l = 0
    w_in_b = w_in[l].astype(BF16)
    back_w = (w_a[l].astype(BF16), w_b[l].astype(BF16), w_o[l].astype(BF16), g_post_mix[l][None], g_pre_ffn[l][None],
              w_gate[l].astype(BF16), w_up[l].astype(BF16), w_down[l].astype(BF16), g_post_ffn[l][None])
    gpre = g_pre_mix[l][None]
    gsgu = g_sgu[l][None]
    in_cols = w_in_b.shape[1]

    xs = x_sample.reshape(n_samples, D_MODEL)
    coef = jnp.repeat(w_spatial[l, :, 0, 0], CHUNK)[None]
    bias = jnp.repeat(b_spatial[l, :, 0], CHUNK)[None]
    full = lambda shape: pl.BlockSpec(shape, lambda *_: (0,) * len(shape))
    s_w = lambda w, dt: jax.ShapeDtypeStruct((n_samples, w), dt)
    a_s, vn_s, q_s, k_s, v_s, ga_s, gb_s = pl.pallas_call(
        _front_sample_kernel,
        grid=(1,),
        in_specs=[full((n_samples, D_MODEL)), full((1, D_MODEL)), full((D_MODEL, in_cols)), full((1, SGU_WIDTH)),
                  full((1, SGU_WIDTH)), full((1, SGU_WIDTH))],
        out_specs=[full((n_samples, SGU_WIDTH)), full((n_samples, SGU_WIDTH)), full((n_samples, ATTN_WIDTH)),
                   full((n_samples, ATTN_WIDTH)), full((n_samples, ATTN_WIDTH)),
                   full((n_samples, D_MODEL)), full((n_samples, D_MODEL))],
        out_shape=[s_w(SGU_WIDTH, BF16), s_w(SGU_WIDTH, F32), s_w(ATTN_WIDTH, F32), s_w(ATTN_WIDTH, F32),
                   s_w(ATTN_WIDTH, F32), s_w(D_MODEL, F32), s_w(D_MODEL, F32)],
        compiler_params=pltpu.CompilerParams(dimension_semantics=("arbitrary",),
                                             vmem_limit_bytes=VMEM_LIMIT_FRONT),
        name="sample_front",
    )(xs, gpre, w_in_b, gsgu, coef, bias)

    ck_t = jnp.transpose(cache_k[l], (0, 2, 3, 1))
    cv_t = jnp.transpose(cache_v[l], (0, 2, 3, 1))
    pt_flat = page_table.reshape(-1)
    slabs_per_page = ATTN_WIDTH // SC_SLAB_ROWS
    ck_rows = ck_t.reshape(n_pool * slabs_per_page, SC_SLAB_ROWS, PAGE_SIZE)
    row_idx = (page_table[:, :, None] * slabs_per_page + jnp.arange(slabs_per_page, dtype=jnp.int32)).reshape(-1)
    q_lanes = jnp.broadcast_to(q_s[:, :, None], (n_samples, ATTN_WIDTH, SC_LANES)).reshape(-1)
    gate_parts = _paged_block_scores(ck_rows, row_idx, q_lanes, n_samples=n_samples, n_pages=n_pages)
    gate_parts = gate_parts.reshape(n_samples, n_past_blk, N_HEADS * SC_LANES)

    n_tiles = seq // TILE_FRONT
    tile3 = lambda w: pl.BlockSpec((1, TILE_FRONT, w), lambda b, j: (b, j, 0))
    tile3t = lambda w: pl.BlockSpec((1, w, TILE_FRONT), lambda b, j: (b, 0, j))
    a_p, qext, kt, vt, vb, ga, gb = pl.pallas_call(
        _front_kernel,
        grid=(bsz, n_tiles),
        in_specs=[tile3(D_MODEL), _resident((1, D_MODEL)), _resident((D_MODEL, in_cols)), _resident((1, SGU_WIDTH)),
                  _resident((SGU_GROUPS, CHUNK, CHUNK)), _resident((CHUNK, SGU_GROUPS))],
        out_specs=[tile3(SGU_WIDTH),
                   pl.BlockSpec((1, N_HEADS, TILE_FRONT, PAIR), lambda b, j: (b, 0, j, 0)),
                   tile3t(ATTN_WIDTH), tile3t(ATTN_WIDTH), tile3(ATTN_WIDTH), tile3(D_MODEL), tile3(D_MODEL)],
        out_shape=[jax.ShapeDtypeStruct((bsz, seq, SGU_WIDTH), BF16),
                   jax.ShapeDtypeStruct((bsz, N_HEADS, seq, PAIR), BF16),
                   jax.ShapeDtypeStruct((bsz, ATTN_WIDTH, seq), F32),
                   jax.ShapeDtypeStruct((bsz, ATTN_WIDTH, seq), F32),
                   jax.ShapeDtypeStruct((bsz, seq, ATTN_WIDTH), BF16),
                   jax.ShapeDtypeStruct((bsz, seq, D_MODEL), F32),
                   jax.ShapeDtypeStruct((bsz, seq, D_MODEL), F32)],
        scratch_shapes=[pltpu.VMEM((n_blk, ATTN_WIDTH), F32)],
        compiler_params=pltpu.CompilerParams(dimension_semantics=("arbitrary", "arbitrary"),
                                             vmem_limit_bytes=VMEM_LIMIT_FRONT),
        name="prompt_front",
    )(x_prompt, gpre, w_in_b, gsgu, w_spatial[l], b_spatial[l].T)

    kext = _key_ext_const(seq)
    att_p = pl.pallas_call(
        _attn_kernel,
        grid=(bsz, N_PAIRS),
        in_specs=[pl.BlockSpec((1, 2, seq, PAIR), lambda b, p: (b, p, 0, 0)),
                  pl.BlockSpec((1, PAIR, seq), lambda b, p: (b, p, 0)),
                  pl.BlockSpec((1, seq, PAIR), lambda b, p: (b, 0, p)),
                  pl.BlockSpec((2, PAIR, seq), lambda b, p: (p, 0, 0))],
        out_specs=pl.BlockSpec((1, seq, PAIR), lambda b, p: (b, 0, p)),
        out_shape=jax.ShapeDtypeStruct((bsz, seq, ATTN_WIDTH), BF16),
        scratch_shapes=[pltpu.VMEM((2, PAIR, seq), BF16)],
        compiler_params=pltpu.CompilerParams(dimension_semantics=("arbitrary", "arbitrary"),
                                             vmem_limit_bytes=VMEM_LIMIT_ATTN),
        name="prompt_moba",
    )(qext, kt, vb, kext)

    n_tok = bsz * seq
    flat = lambda z: z.reshape(n_tok, z.shape[-1])
    y_prompt = _back(flat(x_prompt), flat(a_p), flat(att_p), flat(ga), flat(gb), *back_w).reshape(bsz, seq, D_MODEL)

    idx = pl.pallas_call(
        functools.partial(_topk_kernel, n_past_blk=n_past_blk),
        grid=(n_samples // TOPK_PER_STEP,),
        in_specs=[pl.BlockSpec((TOPK_PER_STEP, n_past_blk, N_HEADS * SC_LANES), lambda i: (i, 0, 0)),
                  pl.BlockSpec((TOPK_PER_STEP, ATTN_WIDTH), lambda i: (i, 0)),
                  pl.BlockSpec((TOPK_PER_STEP, ATTN_WIDTH), lambda i: (i, 0))],
        out_specs=pl.BlockSpec((TOPK_PER_STEP, 8, LANES), lambda i: (i, 0, 0)),
        out_shape=jax.ShapeDtypeStruct((n_samples, 8, LANES), jnp.int32),
        compiler_params=pltpu.CompilerParams(dimension_semantics=("arbitrary",)),
        name="sample_block_topk",
    )(gate_parts, q_s, k_s)
    idx_flat = jnp.transpose(idx[:, :MOBA_TOPK, :N_HEADS], (0, 2, 1)).reshape(-1)
    n_gather = N_HEADS * MOBA_TOPK * PAGES_PER_BLOCK
    att_s = pl.pallas_call(
        functools.partial(_decode_attn_kernel, n_samples=n_samples, n_pages=n_pages, past_len=past_len),
        grid_spec=pltpu.PrefetchScalarGridSpec(
            num_scalar_prefetch=2,
            grid=(n_samples,),
            in_specs=[pl.BlockSpec((n_samples, ATTN_WIDTH), lambda i, pt, ix: (0, 0)),
                      pl.BlockSpec((n_samples, ATTN_WIDTH), lambda i, pt, ix: (0, 0)),
                      pl.BlockSpec((n_samples, ATTN_WIDTH), lambda i, pt, ix: (0, 0)),
                      pl.BlockSpec(memory_space=pl.ANY), pl.BlockSpec(memory_space=pl.ANY)],
            out_specs=pl.BlockSpec((1, 1, ATTN_WIDTH), lambda i, pt, ix: (i, 0, 0)),
            scratch_shapes=[pltpu.VMEM((2, n_gather, HEAD_DIM, PAGE_SIZE), F32),
                            pltpu.VMEM((2, n_gather, HEAD_DIM, PAGE_SIZE), F32),
                            pltpu.SemaphoreType.DMA((2,)), pltpu.SemaphoreType.DMA((2,))]),
        out_shape=jax.ShapeDtypeStruct((n_samples, 1, ATTN_WIDTH), F32),
        compiler_params=pltpu.CompilerParams(dimension_semantics=("arbitrary",),
                                             vmem_limit_bytes=VMEM_LIMIT_DECODE),
        name="sample_moba",
    )(pt_flat, idx_flat, q_s, k_s, v_s, ck_t, cv_t)
    att_s = att_s.reshape(n_samples, ATTN_WIDTH).astype(BF16)

    y_sample = _back(xs, a_s, att_s, ga_s, gb_s, *back_w).reshape(n_samples, 1, D_MODEL)

    heads_t = lambda zt: jnp.transpose(zt.reshape(1, bsz, N_HEADS, HEAD_DIM, seq), (0, 1, 4, 2, 3))
    heads_s = lambda z: z.reshape(1, n_samples, 1, N_HEADS, HEAD_DIM)
    return (y_prompt, y_sample, heads_t(kt), heads_t(vt), heads_s(k_s), heads_s(v_s),
            vn_s.reshape(1, n_samples, 1, SGU_WIDTH))
```

```python
import functools

import numpy as np
import jax
import jax.numpy as jnp
from jax import lax
from jax.experimental import pallas as pl
from jax.experimental.pallas import tpu as pltpu
from jax.experimental.pallas import tpu_sc as plsc

F32 = jnp.float32
BF16 = jnp.bfloat16

D_MODEL = 1024
N_HEADS = 8
HEAD_DIM = 64
ATTN_WIDTH = N_HEADS * HEAD_DIM
MOBA_BLOCK = 256
MOBA_TOPK = 3
CHUNK = 128
SGU_GROUPS = 4
SGU_WIDTH = 512
D_FF = 2816
PAGE_SIZE = 128
EPS = 1e-6
ALIBI_MAX_BIAS = 8.0

LANES = 128
PAIR = 2 * HEAD_DIM
N_PAIRS = N_HEADS // 2
NEG_BIG = -1e30
PAGES_PER_BLOCK = MOBA_BLOCK // PAGE_SIZE
SCALE = HEAD_DIM ** -0.5

TILE_FRONT = MOBA_BLOCK
TILE_BACK = 512
FF_CHUNK = 256
TOPK_PER_STEP = 8

SC_CORES = 2
SC_SUBCORES = 16
SC_WORKERS = SC_CORES * SC_SUBCORES
SC_LANES = 16
SC_SLAB_ROWS = 16
SC_CHUNKS_PER_PAGE = 4
SC_SLABS_PER_CHUNK = ATTN_WIDTH // SC_CHUNKS_PER_PAGE // SC_SLAB_ROWS
SC_ROW_UNROLL = 8
SC_FLUSH_PAGES = 16

VMEM_LIMIT_FRONT = 44 * 1024 * 1024
VMEM_LIMIT_ATTN = 48 * 1024 * 1024
VMEM_LIMIT_BACK = 56 * 1024 * 1024
VMEM_LIMIT_DECODE = 32 * 1024 * 1024


def _slopes():
    return [2.0 ** (-(ALIBI_MAX_BIAS / N_HEADS) * (h + 1)) for h in range(N_HEADS)]


def _rms(x, g):
    return x * lax.rsqrt(jnp.mean(x * x, axis=-1, keepdims=True) + EPS) * g


def _split_bf16(x):
    hi = x.astype(BF16)
    lo = (x - hi.astype(F32)).astype(BF16)
    return hi, lo


def _dot_nt_f32(a, b):
    dn = (((1,), (1,)), ((), ()))
    ah, al = _split_bf16(a)
    bh, bl = _split_bf16(b)
    d = functools.partial(lax.dot_general, dimension_numbers=dn, preferred_element_type=F32)
    return d(ah, bh) + d(ah, bl) + d(al, bh)


def _resident(shape):
    nd = len(shape)
    return pl.BlockSpec(shape, lambda *_: (0,) * nd, pipeline_mode=pl.Buffered(1))


def _front_kernel(x_ref, gpre_ref, win_ref, gsgu_ref, wsp_ref, bspt_ref,
                  a_ref, qext_ref, kt_ref, vt_ref, vb_ref, ga_ref, gb_ref, ksum_ref):
    j = pl.program_id(1)

    @pl.when(j == 0)
    def _():
        ksum_ref[...] = jnp.zeros_like(ksum_ref)

    h = _rms(x_ref[0], gpre_ref[...]).astype(BF16)

    def proj(lo, hi):
        return jnp.dot(h, win_ref[:, lo:hi], preferred_element_type=F32)

    c0 = SGU_WIDTH
    c1 = 2 * SGU_WIDTH
    c2 = c1 + ATTN_WIDTH
    c3 = c2 + ATTN_WIDTH
    c4 = c3 + ATTN_WIDTH
    c5 = c4 + D_MODEL
    u = proj(0, c0)
    v = proj(c0, c1)
    q = proj(c1, c2)
    k = proj(c2, c3)
    vv = proj(c3, c4)
    ga_ref[0] = proj(c4, c5)
    gb_ref[0] = proj(c5, c5 + D_MODEL)

    vn = _rms(v, gsgu_ref[...]).astype(BF16)
    tri_r = lax.broadcasted_iota(jnp.int32, (CHUNK, CHUNK), 0)
    tri_c = lax.broadcasted_iota(jnp.int32, (CHUNK, CHUNK), 1)
    for g in range(SGU_GROUPS):
        wg = jnp.where(tri_c <= tri_r, wsp_ref[g], 0.0).astype(BF16)
        cols = slice(g * CHUNK, (g + 1) * CHUNK)
        for c in range(TILE_FRONT // CHUNK):
            rows = slice(c * CHUNK, (c + 1) * CHUNK)
            sv = jnp.dot(wg, vn[rows, cols], preferred_element_type=F32) + bspt_ref[:, g:g + 1]
            a_ref[0, rows, cols] = (u[rows, cols] * sv).astype(BF16)

    kt_ref[0] = k.T
    vt_ref[0] = vv.T
    vb_ref[0] = vv.astype(BF16)

    ksum_ref[pl.ds(j, 1), :] = jnp.sum(k, axis=0, keepdims=True)
    km = ksum_ref[...] * (1.0 / MOBA_BLOCK)
    n_blk = km.shape[0]
    km_rep = jnp.concatenate([km] * N_HEADS + [jnp.zeros((LANES - N_HEADS * n_blk, ATTN_WIDTH), F32)], axis=0)
    r_i = lax.broadcasted_iota(jnp.int32, (LANES, ATTN_WIDTH), 0)
    c_i = lax.broadcasted_iota(jnp.int32, (LANES, ATTN_WIDTH), 1)
    head_rows = (r_i // n_blk == c_i // HEAD_DIM) & (r_i < N_HEADS * n_blk)
    km_bd = jnp.where(head_rows, km_rep, 0.0)
    gate_t = _dot_nt_f32(km_bd, q)

    blk_i = lax.broadcasted_iota(jnp.int32, (n_blk, TILE_FRONT), 0)
    cand = blk_i < j
    bias_rows = []
    for hh in range(N_HEADS):
        gm = jnp.where(cand, gate_t[hh * n_blk:(hh + 1) * n_blk, :], -jnp.inf)
        rank = jnp.zeros((n_blk, TILE_FRONT), F32)
        for m in range(n_blk):
            gmm = gm[m:m + 1, :]
            ahead = (gmm > gm) | ((gmm == gm) & (blk_i > m))
            rank = rank + jnp.where(ahead, 1.0, 0.0)
        keep = (cand & (rank < float(MOBA_TOPK))) | (blk_i == j)
        bias_rows.append(jnp.where(keep, 0.0, NEG_BIG))
    bias_rows.append(jnp.full((LANES - N_HEADS * n_blk, TILE_FRONT), NEG_BIG, F32))
    sel = jnp.concatenate(bias_rows, axis=0).T
    sel_sw = pltpu.roll(sel, LANES // 2, axis=1)

    lane = lax.broadcasted_iota(jnp.int32, (TILE_FRONT, LANES), 1)
    for hh in range(N_HEADS):
        p = hh // 2
        qp = q[:, p * PAIR:(p + 1) * PAIR] * SCALE
        if hh % 2 == 0:
            lo = HEAD_DIM + hh * n_blk
            ext = jnp.where((lane >= lo) & (lane < lo + n_blk), sel_sw,
                            jnp.where(lane >= LANES - 2, 1.0, 0.0))
            qx = jnp.where(lane < HEAD_DIM, qp, ext)
        else:
            lo = hh * n_blk
            ext = jnp.where((lane >= lo) & (lane < lo + n_blk), sel,
                            jnp.where(lane < 2, 1.0, 0.0))
            qx = jnp.where(lane >= HEAD_DIM, qp, ext)
        qext_ref[0, hh] = qx.astype(BF16)


def _key_ext_const(seq):
    n_blk = seq // MOBA_BLOCK
    pos = np.arange(seq)
    off = (pos % MOBA_BLOCK).astype(np.float32)
    blk = pos // MOBA_BLOCK
    out = np.zeros((N_HEADS, PAIR, seq), np.float32)
    for hh, slope in enumerate(_slopes()):
        if hh % 2 == 0:
            base, r_off, r_blk = HEAD_DIM + hh * n_blk, PAIR - 2, PAIR - 1
        else:
            base, r_off, r_blk = hh * n_blk, 0, 1
        for n in range(n_blk):
            out[hh, base + n] = (blk == n)
        out[hh, r_off] = slope * off
        out[hh, r_blk] = slope * MOBA_BLOCK * blk
    return jnp.asarray(out, dtype=BF16)


def _attn_kernel(qext_ref, kt_ref, vb_ref, kext_ref, o_ref, kx_ref):
    seq = kt_ref.shape[2]
    n_blk = seq // MOBA_BLOCK
    kt = kt_ref[0].astype(BF16)
    row = lax.broadcasted_iota(jnp.int32, (PAIR, seq), 0)
    kx_ref[0] = jnp.where(row < HEAD_DIM, kt, kext_ref[0])
    kx_ref[1] = jnp.where(row >= HEAD_DIM, kt, kext_ref[1])

    r_i = lax.broadcasted_iota(jnp.int32, (MOBA_BLOCK, MOBA_BLOCK), 0)
    c_i = lax.broadcasted_iota(jnp.int32, (MOBA_BLOCK, MOBA_BLOCK), 1)
    causal = c_i <= r_i
    lane = lax.broadcasted_iota(jnp.int32, (MOBA_BLOCK, PAIR), 1)
    for j in range(n_blk):
        w = (j + 1) * MOBA_BLOCK
        rows = slice(j * MOBA_BLOCK, w)
        outs = []
        for e in range(2):
            s = jnp.dot(qext_ref[0, e, rows, :], kx_ref[e, :, :w], preferred_element_type=F32)
            own = jnp.where(causal, s[:, w - MOBA_BLOCK:], NEG_BIG)
            s = own if j == 0 else jnp.concatenate([s[:, :w - MOBA_BLOCK], own], axis=1)
            m = jnp.max(s, axis=-1, keepdims=True)
            p = jnp.exp(s - m)
            l = jnp.sum(p, axis=-1, keepdims=True)
            o = jnp.dot(p.astype(BF16), vb_ref[0, :w, :], preferred_element_type=F32)
            outs.append(o / l)
        o_ref[0, rows, :] = jnp.where(lane < HEAD_DIM, outs[0], outs[1]).astype(BF16)


def _back_kernel(x_ref, a_ref, att_ref, ga_ref, gb_ref, wa_ref, wb_ref, wo_ref, gpm_ref, gpf_ref,
                 wg_ref, wu_ref, wd_ref, gpo_ref, o_ref):
    ya = jnp.dot(a_ref[...], wa_ref[...], preferred_element_type=F32)
    yb = jnp.dot(att_ref[...], wb_ref[...], preferred_element_type=F32)
    m = jax.nn.sigmoid(ga_ref[...]) * ya + jax.nn.sigmoid(gb_ref[...]) * yb
    y = jnp.dot(m.astype(BF16), wo_ref[...], preferred_element_type=F32)
    x1 = x_ref[...] + _rms(y, gpm_ref[...])
    h2 = _rms(x1, gpf_ref[...]).astype(BF16)
    acc = jnp.zeros(x1.shape, F32)
    for c in range(D_FF // FF_CHUNK):
        cols = slice(c * FF_CHUNK, (c + 1) * FF_CHUNK)
        g = jnp.dot(h2, wg_ref[:, cols], preferred_element_type=F32)
        up = jnp.dot(h2, wu_ref[:, cols], preferred_element_type=F32)
        act = (g * jax.nn.sigmoid(g)) * up
        acc = acc + jnp.dot(act.astype(BF16), wd_ref[cols, :], preferred_element_type=F32)
    o_ref[...] = x1 + _rms(acc, gpo_ref[...])


def _back(x, a, att, ga, gb, w_a, w_b, w_o, gpm, gpf, w_g, w_u, w_d, gpo):
    n = x.shape[0]
    tm = min(TILE_BACK, n)
    row = lambda w: pl.BlockSpec((tm, w), lambda i: (i, 0))
    return pl.pallas_call(
        _back_kernel,
        grid=(n // tm,),
        in_specs=[row(D_MODEL), row(SGU_WIDTH), row(ATTN_WIDTH), row(D_MODEL), row(D_MODEL),
                  _resident(w_a.shape), _resident(w_b.shape), _resident(w_o.shape),
                  _resident(gpm.shape), _resident(gpf.shape),
                  _resident(w_g.shape), _resident(w_u.shape), _resident(w_d.shape), _resident(gpo.shape)],
        out_specs=row(D_MODEL),
        out_shape=jax.ShapeDtypeStruct((n, D_MODEL), F32),
        compiler_params=pltpu.CompilerParams(dimension_semantics=("arbitrary",),
                                             vmem_limit_bytes=VMEM_LIMIT_BACK),
        name="mixer_back_ffn",
    )(x, a, att, ga, gb, w_a, w_b, w_o, gpm, gpf, w_g, w_u, w_d, gpo)


def _front_sample_kernel(x_ref, gpre_ref, win_ref, gsgu_ref, coef_ref, bias_ref,
                         a_ref, vn_ref, q_ref, k_ref, v_ref, ga_ref, gb_ref):
    h = _rms(x_ref[...], gpre_ref[...]).astype(BF16)

    def proj(lo, hi):
        return jnp.dot(h, win_ref[:, lo:hi], preferred_element_type=F32)

    c0 = SGU_WIDTH
    c1 = 2 * SGU_WIDTH
    c2 = c1 + ATTN_WIDTH
    c3 = c2 + ATTN_WIDTH
    c4 = c3 + ATTN_WIDTH
    c5 = c4 + D_MODEL
    u = proj(0, c0)
    vn = _rms(proj(c0, c1), gsgu_ref[...])
    vn_ref[...] = vn
    a_ref[...] = (u * (vn * coef_ref[...] + bias_ref[...])).astype(BF16)
    q_ref[...] = proj(c1, c2)
    k_ref[...] = proj(c2, c3)
    v_ref[...] = proj(c3, c4)
    ga_ref[...] = proj(c4, c5)
    gb_ref[...] = proj(c5, c5 + D_MODEL)


def _head_sums(row):
    return [jnp.sum(row[:, h * HEAD_DIM:(h + 1) * HEAD_DIM], axis=1, keepdims=True) for h in range(N_HEADS)]


def _tree_sum(xs):
    while len(xs) > 1:
        xs = [xs[i] + xs[i + 1] for i in range(0, len(xs) - 1, 2)] + ([xs[-1]] if len(xs) % 2 else [])
    return xs[0]


def _paged_key_scores(ck_rows, row_idx, q_lanes, *, n_samples, n_pages):
    chunks = n_pages * SC_CHUNKS_PER_PAGE
    per_worker = n_samples // SC_WORKERS
    idx_len = chunks * SC_SLABS_PER_CHUNK
    q_len = ATTN_WIDTH * SC_LANES
    page_len = N_HEADS * PAGE_SIZE
    flush_len = SC_FLUSH_PAGES * page_len
    sample_len = n_pages * page_len
    rows_per_chunk = SC_SLABS_PER_CHUNK * SC_SLAB_ROWS
    key_chunks = PAGE_SIZE // SC_LANES
    assert SC_SLAB_ROWS % SC_ROW_UNROLL == 0 and HEAD_DIM % SC_ROW_UNROLL == 0 and n_pages % SC_FLUSH_PAGES == 0
    ring = SC_CHUNKS_PER_PAGE
    mesh = plsc.VectorSubcoreMesh(core_axis_name="c", subcore_axis_name="s")

    @functools.partial(
        pl.kernel, mesh=mesh,
        out_type=jax.ShapeDtypeStruct((n_samples * sample_len,), F32),
        scratch_types=[pltpu.VMEM((idx_len,), jnp.int32),
                       pltpu.VMEM((q_len,), F32),
                       pltpu.VMEM((ring, SC_SLABS_PER_CHUNK, SC_SLAB_ROWS, PAGE_SIZE), F32),
                       pltpu.VMEM((flush_len,), F32),
                       pltpu.SemaphoreType.DMA((ring,))],
        name="sample_key_scores_sc",
    )
    def scores(ck_hbm, idx_hbm, q_hbm, out_hbm, idx_v, q_v, rows_v, out_v, sems):
        wid = lax.axis_index("s") * SC_CORES + lax.axis_index("c")

        def fetch(c, slot):
            ids = idx_v.at[pl.ds(pl.multiple_of(c * SC_SLABS_PER_CHUNK, SC_SLABS_PER_CHUNK), SC_SLABS_PER_CHUNK)]
            return pltpu.make_async_copy(ck_hbm.at[ids], rows_v.at[slot], sems.at[slot])

        @pl.loop(0, per_worker)
        def _(t):
            n = wid * per_worker + t
            pltpu.sync_copy(idx_hbm.at[pl.ds(pl.multiple_of(n * idx_len, idx_len), idx_len)], idx_v)
            pltpu.sync_copy(q_hbm.at[pl.ds(pl.multiple_of(n * q_len, q_len), q_len)], q_v)
            for slot in range(ring):
                fetch(slot, slot).start()

            @pl.loop(0, n_pages)
            def _(page):
                c0 = page * ring
                page_in_flush = page % SC_FLUSH_PAGES

                @pl.when(page_in_flush == 0)
                def _():
                    @pl.loop(0, flush_len, step=SC_LANES)
                    def _(o):
                        out_v[pl.ds(pl.multiple_of(o, SC_LANES), SC_LANES)] = jnp.zeros((SC_LANES,), F32)

                for slot in range(ring):
                    fetch(c0 + slot, slot).wait()

                    @pl.loop(0, rows_per_chunk, step=SC_ROW_UNROLL)
                    def _(r0, slot=slot):
                        row0 = slot * rows_per_chunk + r0
                        obase = (page_in_flush * N_HEADS + row0 // HEAD_DIM) * PAGE_SIZE
                        slab = r0 // SC_SLAB_ROWS
                        j0 = r0 % SC_SLAB_ROWS
                        qs = [q_v[pl.ds(pl.multiple_of((row0 + jj) * SC_LANES, SC_LANES), SC_LANES)]
                              for jj in range(SC_ROW_UNROLL)]
                        sums = []
                        for kk in range(key_chunks):
                            terms = [rows_v[slot, slab, j0 + jj, pl.ds(kk * SC_LANES, SC_LANES)] * qs[jj]
                                     for jj in range(SC_ROW_UNROLL)]
                            sums.append(_tree_sum(terms))
                        for kk in range(key_chunks):
                            ooff = pl.multiple_of(obase + kk * SC_LANES, SC_LANES)
                            plsc.addupdate(out_v.at[pl.ds(ooff, SC_LANES)], sums[kk])

                    @pl.when(c0 + slot + ring < chunks)
                    def _():
                        fetch(c0 + slot + ring, slot).start()

                @pl.when(page_in_flush == SC_FLUSH_PAGES - 1)
                def _():
                    off = n * sample_len + (page - (SC_FLUSH_PAGES - 1)) * page_len
                    pltpu.sync_copy(out_v, out_hbm.at[pl.ds(pl.multiple_of(off, flush_len), flush_len)])

    return scores(ck_rows, row_idx, q_lanes)


def _split3_bf16(x):
    hi = x.astype(BF16)
    r1 = x - hi.astype(F32)
    mid = r1.astype(BF16)
    lo = (r1 - mid.astype(F32)).astype(BF16)
    return hi, mid, lo


def _dot_exact_onehot(x, onehot):
    return sum(jnp.dot(t, onehot, preferred_element_type=F32) for t in _split3_bf16(x))


def _topk_kernel(s_ref, q_ref, k_ref, idx_ref, *, n_past_blk):
    per_step = s_ref.shape[0]
    sub = 8
    s = s_ref[...].reshape(per_step * n_past_blk, PAGES_PER_BLOCK, N_HEADS, PAGE_SIZE)
    per_blk = s[:, 0]
    for pg in range(1, PAGES_PER_BLOCK):
        per_blk = per_blk + s[:, pg]
    ones = jnp.ones((PAGE_SIZE, LANES), BF16)
    sums = _dot_exact_onehot(per_blk.reshape(per_step * n_past_blk * N_HEADS, PAGE_SIZE), ones)
    h_i = lax.broadcasted_iota(jnp.int32, (N_HEADS, LANES), 0)
    l_i = lax.broadcasted_iota(jnp.int32, (N_HEADS, LANES), 1)
    diag = jnp.where(h_i == l_i, 1.0, 0.0)
    gates = jnp.sum(sums.reshape(per_step * n_past_blk, N_HEADS, LANES) * diag, axis=1)
    r_w = lax.broadcasted_iota(jnp.int32, (ATTN_WIDTH, LANES), 0)
    c_w = lax.broadcasted_iota(jnp.int32, (ATTN_WIDTH, LANES), 1)
    head_sum = jnp.where(r_w // HEAD_DIM == c_w, 1.0, 0.0).astype(BF16)
    own = _dot_exact_onehot(q_ref[...] * k_ref[...], head_sum)
    blk_i = lax.broadcasted_iota(jnp.int32, (n_past_blk + sub, LANES), 0)
    blk_f = blk_i.astype(F32)
    out_row = lax.broadcasted_iota(jnp.int32, (sub, LANES), 0)
    for i in range(per_step):
        gate = jnp.concatenate([gates[i * n_past_blk:(i + 1) * n_past_blk],
                                jnp.broadcast_to(own[i:i + 1], (sub, LANES))], axis=0) * (1.0 / MOBA_BLOCK)
        gm = jnp.where(blk_i < n_past_blk, gate, -jnp.inf)
        out = jnp.zeros((sub, LANES), F32)
        for r in range(MOBA_TOPK):
            mx = jnp.max(gm, axis=0, keepdims=True)
            pick = jnp.min(jnp.where(gm == mx, blk_f, float(LANES)), axis=0, keepdims=True)
            out = jnp.where(out_row == r, pick, out)
            gm = jnp.where(blk_f == pick, -jnp.inf, gm)
        idx_ref[i] = out.astype(jnp.int32)


def _decode_attn_kernel(pt_ref, idx_ref, s_ref, q_ref, k_ref, v_ref, cv_ref, o_ref,
                        vbuf_ref, vsem_ref, *, n_samples, n_pages, past_len):
    n = pl.program_id(0)
    n_tiles = MOBA_TOPK * PAGES_PER_BLOCK
    per_sample = N_HEADS * n_tiles

    def pages(sample, h, j):
        blk = idx_ref[sample * (N_HEADS * MOBA_TOPK) + h * MOBA_TOPK + j]
        return [blk * PAGES_PER_BLOCK + pg for pg in range(PAGES_PER_BLOCK)]

    def copies(sample, slot):
        out = []
        for h in range(N_HEADS):
            for j in range(MOBA_TOPK):
                for pg, page in enumerate(pages(sample, h, j)):
                    phys = pt_ref[sample * n_pages + page]
                    i = (h * MOBA_TOPK + j) * PAGES_PER_BLOCK + pg
                    out.append(pltpu.make_async_copy(cv_ref.at[phys, h], vbuf_ref.at[slot, i], vsem_ref.at[slot]))
        return out

    @pl.when(n == 0)
    def _():
        for cp in copies(0, 0):
            cp.start()

    @pl.when(n + 1 < n_samples)
    def _():
        for cp in copies(n + 1, (n + 1) % 2):
            cp.start()

    slot = n % 2
    for i in range(per_sample):
        pltpu.make_async_copy(cv_ref.at[0, 0], vbuf_ref.at[slot, i], vsem_ref.at[slot]).wait()

    q_row = q_ref[pl.ds(n, 1), :]
    k_row = k_ref[pl.ds(n, 1), :]
    v_row = v_ref[pl.ds(n, 1), :]
    s_new = _head_sums(q_row * k_row)
    lane_row = lax.broadcasted_iota(jnp.int32, (1, PAGE_SIZE), 1)
    lane_w = lax.broadcasted_iota(jnp.int32, (1, ATTN_WIDTH), 1)
    acc_heads = []
    p_new_row = jnp.zeros((1, ATTN_WIDTH), F32)
    for h, slope in enumerate(_slopes()):
        scores = []
        for j in range(MOBA_TOPK):
            for page in pages(n, h, j):
                s = s_ref[0, page, pl.ds(h, 1), :] * SCALE
                key_pos = page * PAGE_SIZE + lane_row
                scores.append(s - slope * (past_len - key_pos).astype(F32))
        s_n = s_new[h] * SCALE
        m = s_n
        for s in scores:
            m = jnp.maximum(m, jnp.max(s, axis=1, keepdims=True))
        p_n = jnp.exp(s_n - m)
        l = p_n
        acc = jnp.zeros((HEAD_DIM, PAGE_SIZE), F32)
        for t, s in enumerate(scores):
            p = jnp.exp(s - m)
            l = l + jnp.sum(p, axis=1, keepdims=True)
            acc = acc + vbuf_ref[slot, h * n_tiles + t] * p
        acc_heads.append(acc / l)
        p_new_row = jnp.where(lane_w // HEAD_DIM == h, p_n / l, p_new_row)
    acc_all = jnp.concatenate(acc_heads, axis=0)
    o_row = jnp.sum(acc_all.T, axis=0, keepdims=True) + p_new_row * v_row
    o_ref[0] = o_row


def kernel(x_prompt, x_sample, cache_k, cache_v, page_table, g_pre_mix, w_in, g_sgu, w_spatial, b_spatial,
           w_a, w_b, w_o, g_post_mix, g_pre_ffn, w_gate, w_up, w_down, g_post_ffn):
    depth = w_in.shape[0]
    assert depth == 1, "kernels are written for a single layer"
    bsz, seq, _ = x_prompt.shape
    n_samples, dec_seq, _ = x_sample.shape
    assert dec_seq == 1
    n_pool = cache_k.shape[1]
    n_pages = page_table.shape[1]
    past_len = n_pages * PAGE_SIZE
    n_past_blk = past_len // MOBA_BLOCK
    n_blk = seq // MOBA_BLOCK
    assert n_blk * N_HEADS <= HEAD_DIM and n_past_blk < LANES
    assert n_samples % SC_WORKERS == 0 and n_samples % TOPK_PER_STEP == 0 and n_pages % PAGES_PER_BLOCK == 0

    l = 0
    w_in_b = w_in[l].astype(BF16)
    back_w = (w_a[l].astype(BF16), w_b[l].astype(BF16), w_o[l].astype(BF16), g_post_mix[l][None], g_pre_ffn[l][None],
              w_gate[l].astype(BF16), w_up[l].astype(BF16), w_down[l].astype(BF16), g_post_ffn[l][None])
    gpre = g_pre_mix[l][None]
    gsgu = g_sgu[l][None]
    in_cols = w_in_b.shape[1]

    xs = x_sample.reshape(n_samples, D_MODEL)
    coef = jnp.repeat(w_spatial[l, :, 0, 0], CHUNK)[None]
    bias = jnp.repeat(b_spatial[l, :, 0], CHUNK)[None]
    full = lambda shape: pl.BlockSpec(shape, lambda *_: (0,) * len(shape))
    s_w = lambda w, dt: jax.ShapeDtypeStruct((n_samples, w), dt)
    a_s, vn_s, q_s, k_s, v_s, ga_s, gb_s = pl.pallas_call(
        _front_sample_kernel,
        grid=(1,),
        in_specs=[full((n_samples, D_MODEL)), full((1, D_MODEL)), full((D_MODEL, in_cols)), full((1, SGU_WIDTH)),
                  full((1, SGU_WIDTH)), full((1, SGU_WIDTH))],
        out_specs=[full((n_samples, SGU_WIDTH)), full((n_samples, SGU_WIDTH)), full((n_samples, ATTN_WIDTH)),
                   full((n_samples, ATTN_WIDTH)), full((n_samples, ATTN_WIDTH)),
                   full((n_samples, D_MODEL)), full((n_samples, D_MODEL))],
        out_shape=[s_w(SGU_WIDTH, BF16), s_w(SGU_WIDTH, F32), s_w(ATTN_WIDTH, F32), s_w(ATTN_WIDTH, F32),
                   s_w(ATTN_WIDTH, F32), s_w(D_MODEL, F32), s_w(D_MODEL, F32)],
        compiler_params=pltpu.CompilerParams(dimension_semantics=("arbitrary",),
                                             vmem_limit_bytes=VMEM_LIMIT_FRONT),
        name="sample_front",
    )(xs, gpre, w_in_b, gsgu, coef, bias)

    ck_t = jnp.transpose(cache_k[l], (0, 2, 3, 1))
    cv_t = jnp.transpose(cache_v[l], (0, 2, 3, 1))
    pt_flat = page_table.reshape(-1)
    slabs_per_page = ATTN_WIDTH // SC_SLAB_ROWS
    ck_rows = ck_t.reshape(n_pool * slabs_per_page, SC_SLAB_ROWS, PAGE_SIZE)
    row_idx = (page_table[:, :, None] * slabs_per_page + jnp.arange(slabs_per_page, dtype=jnp.int32)).reshape(-1)
    q_lanes = jnp.broadcast_to(q_s[:, :, None], (n_samples, ATTN_WIDTH, SC_LANES)).reshape(-1)
    key_scores = _paged_key_scores(ck_rows, row_idx, q_lanes, n_samples=n_samples, n_pages=n_pages)
    key_scores = key_scores.reshape(n_samples, n_pages, N_HEADS, PAGE_SIZE)

    n_tiles = seq // TILE_FRONT
    tile3 = lambda w: pl.BlockSpec((1, TILE_FRONT, w), lambda b, j: (b, j, 0))
    tile3t = lambda w: pl.BlockSpec((1, w, TILE_FRONT), lambda b, j: (b, 0, j))
    a_p, qext, kt, vt, vb, ga, gb = pl.pallas_call(
        _front_kernel,
        grid=(bsz, n_tiles),
        in_specs=[tile3(D_MODEL), _resident((1, D_MODEL)), _resident((D_MODEL, in_cols)), _resident((1, SGU_WIDTH)),
                  _resident((SGU_GROUPS, CHUNK, CHUNK)), _resident((CHUNK, SGU_GROUPS))],
        out_specs=[tile3(SGU_WIDTH),
                   pl.BlockSpec((1, N_HEADS, TILE_FRONT, PAIR), lambda b, j: (b, 0, j, 0)),
                   tile3t(ATTN_WIDTH), tile3t(ATTN_WIDTH), tile3(ATTN_WIDTH), tile3(D_MODEL), tile3(D_MODEL)],
        out_shape=[jax.ShapeDtypeStruct((bsz, seq, SGU_WIDTH), BF16),
                   jax.ShapeDtypeStruct((bsz, N_HEADS, seq, PAIR), BF16),
                   jax.ShapeDtypeStruct((bsz, ATTN_WIDTH, seq), F32),
                   jax.ShapeDtypeStruct((bsz, ATTN_WIDTH, seq), F32),
                   jax.ShapeDtypeStruct((bsz, seq, ATTN_WIDTH), BF16),
                   jax.ShapeDtypeStruct((bsz, seq, D_MODEL), F32),
                   jax.ShapeDtypeStruct((bsz, seq, D_MODEL), F32)],
        scratch_shapes=[pltpu.VMEM((n_blk, ATTN_WIDTH), F32)],
        compiler_params=pltpu.CompilerParams(dimension_semantics=("arbitrary", "arbitrary"),
                                             vmem_limit_bytes=VMEM_LIMIT_FRONT),
        name="prompt_front",
    )(x_prompt, gpre, w_in_b, gsgu, w_spatial[l], b_spatial[l].T)

    kext = _key_ext_const(seq)
    att_p = pl.pallas_call(
        _attn_kernel,
        grid=(bsz, N_PAIRS),
        in_specs=[pl.BlockSpec((1, 2, seq, PAIR), lambda b, p: (b, p, 0, 0)),
                  pl.BlockSpec((1, PAIR, seq), lambda b, p: (b, p, 0)),
                  pl.BlockSpec((1, seq, PAIR), lambda b, p: (b, 0, p)),
                  pl.BlockSpec((2, PAIR, seq), lambda b, p: (p, 0, 0))],
        out_specs=pl.BlockSpec((1, seq, PAIR), lambda b, p: (b, 0, p)),
        out_shape=jax.ShapeDtypeStruct((bsz, seq, ATTN_WIDTH), BF16),
        scratch_shapes=[pltpu.VMEM((2, PAIR, seq), BF16)],
        compiler_params=pltpu.CompilerParams(dimension_semantics=("arbitrary", "arbitrary"),
                                             vmem_limit_bytes=VMEM_LIMIT_ATTN),
        name="prompt_moba",
    )(qext, kt, vb, kext)

    n_tok = bsz * seq
    flat = lambda z: z.reshape(n_tok, z.shape[-1])
    y_prompt = _back(flat(x_prompt), flat(a_p), flat(att_p), flat(ga), flat(gb), *back_w).reshape(bsz, seq, D_MODEL)

    idx = pl.pallas_call(
        functools.partial(_topk_kernel, n_past_blk=n_past_blk),
        grid=(n_samples // TOPK_PER_STEP,),
        in_specs=[pl.BlockSpec((TOPK_PER_STEP, n_pages, N_HEADS, PAGE_SIZE), lambda i: (i, 0, 0, 0)),
                  pl.BlockSpec((TOPK_PER_STEP, ATTN_WIDTH), lambda i: (i, 0)),
                  pl.BlockSpec((TOPK_PER_STEP, ATTN_WIDTH), lambda i: (i, 0))],
        out_specs=pl.BlockSpec((TOPK_PER_STEP, 8, LANES), lambda i: (i, 0, 0)),
        out_shape=jax.ShapeDtypeStruct((n_samples, 8, LANES), jnp.int32),
        compiler_params=pltpu.CompilerParams(dimension_semantics=("arbitrary",)),
        name="sample_block_topk",
    )(key_scores, q_s, k_s)
    idx_flat = jnp.transpose(idx[:, :MOBA_TOPK, :N_HEADS], (0, 2, 1)).reshape(-1)
    n_gather = N_HEADS * MOBA_TOPK * PAGES_PER_BLOCK
    att_s = pl.pallas_call(
        functools.partial(_decode_attn_kernel, n_samples=n_samples, n_pages=n_pages, past_len=past_len),
        grid_spec=pltpu.PrefetchScalarGridSpec(
            num_scalar_prefetch=2,
            grid=(n_samples,),
            in_specs=[pl.BlockSpec((1, n_pages, N_HEADS, PAGE_SIZE), lambda i, pt, ix: (i, 0, 0, 0)),
                      pl.BlockSpec((n_samples, ATTN_WIDTH), lambda i, pt, ix: (0, 0)),
                      pl.BlockSpec((n_samples, ATTN_WIDTH), lambda i, pt, ix: (0, 0)),
                      pl.BlockSpec((n_samples, ATTN_WIDTH), lambda i, pt, ix: (0, 0)),
                      pl.BlockSpec(memory_space=pl.ANY)],
            out_specs=pl.BlockSpec((1, 1, ATTN_WIDTH), lambda i, pt, ix: (i, 0, 0)),
            scratch_shapes=[pltpu.VMEM((2, n_gather, HEAD_DIM, PAGE_SIZE), F32),
                            pltpu.SemaphoreType.DMA((2,))]),
        out_shape=jax.ShapeDtypeStruct((n_samples, 1, ATTN_WIDTH), F32),
        compiler_params=pltpu.CompilerParams(dimension_semantics=("arbitrary",),
                                             vmem_limit_bytes=VMEM_LIMIT_DECODE),
        name="sample_moba",
    )(pt_flat, idx_flat, key_scores, q_s, k_s, v_s, cv_t)
    att_s = att_s.reshape(n_samples, ATTN_WIDTH).astype(BF16)

    y_sample = _back(xs, a_s, att_s, ga_s, gb_s, *back_w).reshape(n_samples, 1, D_MODEL)

    heads_t = lambda zt: jnp.transpose(zt.reshape(1, bsz, N_HEADS, HEAD_DIM, seq), (0, 1, 4, 2, 3))
    heads_s = lambda z: z.reshape(1, n_samples, 1, N_HEADS, HEAD_DIM)
    return (y_prompt, y_sample, heads_t(kt), heads_t(vt), heads_s(k_s), heads_s(v_s),
            vn_s.reshape(1, n_samples, 1, SGU_WIDTH))
```

```python
import functools

import numpy as np
import jax
import jax.numpy as jnp
from jax import lax
from jax.experimental import pallas as pl
from jax.experimental.pallas import tpu as pltpu
from jax.experimental.pallas import tpu_sc as plsc

F32 = jnp.float32
BF16 = jnp.bfloat16

D_MODEL = 1024
N_HEADS = 8
HEAD_DIM = 64
ATTN_WIDTH = N_HEADS * HEAD_DIM
MOBA_BLOCK = 256
MOBA_TOPK = 3
CHUNK = 128
SGU_GROUPS = 4
SGU_WIDTH = 512
D_FF = 2816
PAGE_SIZE = 128
EPS = 1e-6
ALIBI_MAX_BIAS = 8.0

LANES = 128
PAIR = 2 * HEAD_DIM
N_PAIRS = N_HEADS // 2
NEG_BIG = -1e30
PAGES_PER_BLOCK = MOBA_BLOCK // PAGE_SIZE
SCALE = HEAD_DIM ** -0.5

TILE_FRONT = MOBA_BLOCK
TILE_BACK = 512
FF_CHUNK = 256
TOPK_PER_STEP = 16

SC_CORES = 2
SC_SUBCORES = 16
SC_WORKERS = SC_CORES * SC_SUBCORES
SC_LANES = 16
SC_SLAB_ROWS = 16
SC_CHUNKS_PER_PAGE = 4
SC_SLABS_PER_CHUNK = ATTN_WIDTH // SC_CHUNKS_PER_PAGE // SC_SLAB_ROWS
SC_ROW_UNROLL = 8
SC_FLUSH_PAGES = 16

VMEM_LIMIT_FRONT = 44 * 1024 * 1024
VMEM_LIMIT_ATTN = 48 * 1024 * 1024
VMEM_LIMIT_BACK = 56 * 1024 * 1024
VMEM_LIMIT_DECODE = 32 * 1024 * 1024


def _slopes():
    return [2.0 ** (-(ALIBI_MAX_BIAS / N_HEADS) * (h + 1)) for h in range(N_HEADS)]


def _rms(x, g):
    return x * lax.rsqrt(jnp.mean(x * x, axis=-1, keepdims=True) + EPS) * g


def _split_bf16(x):
    hi = x.astype(BF16)
    lo = (x - hi.astype(F32)).astype(BF16)
    return hi, lo


def _dot_nt_f32(a, b):
    dn = (((1,), (1,)), ((), ()))
    ah, al = _split_bf16(a)
    bh, bl = _split_bf16(b)
    d = functools.partial(lax.dot_general, dimension_numbers=dn, preferred_element_type=F32)
    return d(ah, bh) + d(ah, bl) + d(al, bh)


def _resident(shape):
    nd = len(shape)
    return pl.BlockSpec(shape, lambda *_: (0,) * nd, pipeline_mode=pl.Buffered(1))


def _front_kernel(x_ref, gpre_ref, win_ref, gsgu_ref, wsp_ref, bspt_ref,
                  a_ref, qext_ref, kt_ref, vt_ref, vb_ref, ga_ref, gb_ref, ksum_ref):
    j = pl.program_id(1)

    @pl.when(j == 0)
    def _():
        ksum_ref[...] = jnp.zeros_like(ksum_ref)

    h = _rms(x_ref[0], gpre_ref[...]).astype(BF16)

    def proj(lo, hi):
        return jnp.dot(h, win_ref[:, lo:hi], preferred_element_type=F32)

    c0 = SGU_WIDTH
    c1 = 2 * SGU_WIDTH
    c2 = c1 + ATTN_WIDTH
    c3 = c2 + ATTN_WIDTH
    c4 = c3 + ATTN_WIDTH
    c5 = c4 + D_MODEL
    u = proj(0, c0)
    v = proj(c0, c1)
    q = proj(c1, c2)
    k = proj(c2, c3)
    vv = proj(c3, c4)
    ga_ref[0] = proj(c4, c5)
    gb_ref[0] = proj(c5, c5 + D_MODEL)

    vn = _rms(v, gsgu_ref[...]).astype(BF16)
    tri_r = lax.broadcasted_iota(jnp.int32, (CHUNK, CHUNK), 0)
    tri_c = lax.broadcasted_iota(jnp.int32, (CHUNK, CHUNK), 1)
    for g in range(SGU_GROUPS):
        wg = jnp.where(tri_c <= tri_r, wsp_ref[g], 0.0).astype(BF16)
        cols = slice(g * CHUNK, (g + 1) * CHUNK)
        for c in range(TILE_FRONT // CHUNK):
            rows = slice(c * CHUNK, (c + 1) * CHUNK)
            sv = jnp.dot(wg, vn[rows, cols], preferred_element_type=F32) + bspt_ref[:, g:g + 1]
            a_ref[0, rows, cols] = (u[rows, cols] * sv).astype(BF16)

    kt_ref[0] = k.T
    vt_ref[0] = vv.T
    vb_ref[0] = vv.astype(BF16)

    ksum_ref[pl.ds(j, 1), :] = jnp.sum(k, axis=0, keepdims=True)
    km = ksum_ref[...] * (1.0 / MOBA_BLOCK)
    n_blk = km.shape[0]
    km_rep = jnp.concatenate([km] * N_HEADS + [jnp.zeros((LANES - N_HEADS * n_blk, ATTN_WIDTH), F32)], axis=0)
    r_i = lax.broadcasted_iota(jnp.int32, (LANES, ATTN_WIDTH), 0)
    c_i = lax.broadcasted_iota(jnp.int32, (LANES, ATTN_WIDTH), 1)
    head_rows = (r_i // n_blk == c_i // HEAD_DIM) & (r_i < N_HEADS * n_blk)
    km_bd = jnp.where(head_rows, km_rep, 0.0)
    gate_t = _dot_nt_f32(km_bd, q)

    blk_i = lax.broadcasted_iota(jnp.int32, (n_blk, TILE_FRONT), 0)
    cand = blk_i < j
    bias_rows = []
    for hh in range(N_HEADS):
        gm = jnp.where(cand, gate_t[hh * n_blk:(hh + 1) * n_blk, :], -jnp.inf)
        rank = jnp.zeros((n_blk, TILE_FRONT), F32)
        for m in range(n_blk):
            gmm = gm[m:m + 1, :]
            ahead = (gmm > gm) | ((gmm == gm) & (blk_i > m))
            rank = rank + jnp.where(ahead, 1.0, 0.0)
        keep = (cand & (rank < float(MOBA_TOPK))) | (blk_i == j)
        bias_rows.append(jnp.where(keep, 0.0, NEG_BIG))
    bias_rows.append(jnp.full((LANES - N_HEADS * n_blk, TILE_FRONT), NEG_BIG, F32))
    sel = jnp.concatenate(bias_rows, axis=0).T
    sel_sw = pltpu.roll(sel, LANES // 2, axis=1)

    lane = lax.broadcasted_iota(jnp.int32, (TILE_FRONT, LANES), 1)
    for hh in range(N_HEADS):
        p = hh // 2
        qp = q[:, p * PAIR:(p + 1) * PAIR] * SCALE
        if hh % 2 == 0:
            lo = HEAD_DIM + hh * n_blk
            ext = jnp.where((lane >= lo) & (lane < lo + n_blk), sel_sw,
                            jnp.where(lane >= LANES - 2, 1.0, 0.0))
            qx = jnp.where(lane < HEAD_DIM, qp, ext)
        else:
            lo = hh * n_blk
            ext = jnp.where((lane >= lo) & (lane < lo + n_blk), sel,
                            jnp.where(lane < 2, 1.0, 0.0))
            qx = jnp.where(lane >= HEAD_DIM, qp, ext)
        qext_ref[0, hh] = qx.astype(BF16)


def _key_ext_const(seq):
    n_blk = seq // MOBA_BLOCK
    pos = np.arange(seq)
    off = (pos % MOBA_BLOCK).astype(np.float32)
    blk = pos // MOBA_BLOCK
    out = np.zeros((N_HEADS, PAIR, seq), np.float32)
    for hh, slope in enumerate(_slopes()):
        if hh % 2 == 0:
            base, r_off, r_blk = HEAD_DIM + hh * n_blk, PAIR - 2, PAIR - 1
        else:
            base, r_off, r_blk = hh * n_blk, 0, 1
        for n in range(n_blk):
            out[hh, base + n] = (blk == n)
        out[hh, r_off] = slope * off
        out[hh, r_blk] = slope * MOBA_BLOCK * blk
    return jnp.asarray(out, dtype=BF16)


def _attn_kernel(qext_ref, kt_ref, vb_ref, kext_ref, o_ref, kx_ref):
    seq = kt_ref.shape[2]
    n_blk = seq // MOBA_BLOCK
    kt = kt_ref[0].astype(BF16)
    row = lax.broadcasted_iota(jnp.int32, (PAIR, seq), 0)
    kx_ref[0] = jnp.where(row < HEAD_DIM, kt, kext_ref[0])
    kx_ref[1] = jnp.where(row >= HEAD_DIM, kt, kext_ref[1])

    r_i = lax.broadcasted_iota(jnp.int32, (MOBA_BLOCK, MOBA_BLOCK), 0)
    c_i = lax.broadcasted_iota(jnp.int32, (MOBA_BLOCK, MOBA_BLOCK), 1)
    causal = c_i <= r_i
    lane = lax.broadcasted_iota(jnp.int32, (MOBA_BLOCK, PAIR), 1)
    for j in range(n_blk):
        w = (j + 1) * MOBA_BLOCK
        rows = slice(j * MOBA_BLOCK, w)
        outs = []
        for e in range(2):
            s = jnp.dot(qext_ref[0, e, rows, :], kx_ref[e, :, :w], preferred_element_type=F32)
            own = jnp.where(causal, s[:, w - MOBA_BLOCK:], NEG_BIG)
            s = own if j == 0 else jnp.concatenate([s[:, :w - MOBA_BLOCK], own], axis=1)
            m = jnp.max(s, axis=-1, keepdims=True)
            p = jnp.exp(s - m)
            l = jnp.sum(p, axis=-1, keepdims=True)
            o = jnp.dot(p.astype(BF16), vb_ref[0, :w, :], preferred_element_type=F32)
            outs.append(o / l)
        o_ref[0, rows, :] = jnp.where(lane < HEAD_DIM, outs[0], outs[1]).astype(BF16)


def _back_kernel(x_ref, a_ref, att_ref, ga_ref, gb_ref, wa_ref, wb_ref, wo_ref, gpm_ref, gpf_ref,
                 wg_ref, wu_ref, wd_ref, gpo_ref, o_ref):
    ya = jnp.dot(a_ref[...], wa_ref[...], preferred_element_type=F32)
    yb = jnp.dot(att_ref[...], wb_ref[...], preferred_element_type=F32)
    m = jax.nn.sigmoid(ga_ref[...]) * ya + jax.nn.sigmoid(gb_ref[...]) * yb
    y = jnp.dot(m.astype(BF16), wo_ref[...], preferred_element_type=F32)
    x1 = x_ref[...] + _rms(y, gpm_ref[...])
    h2 = _rms(x1, gpf_ref[...]).astype(BF16)
    acc = jnp.zeros(x1.shape, F32)
    for c in range(D_FF // FF_CHUNK):
        cols = slice(c * FF_CHUNK, (c + 1) * FF_CHUNK)
        g = jnp.dot(h2, wg_ref[:, cols], preferred_element_type=F32)
        up = jnp.dot(h2, wu_ref[:, cols], preferred_element_type=F32)
        act = (g * jax.nn.sigmoid(g)) * up
        acc = acc + jnp.dot(act.astype(BF16), wd_ref[cols, :], preferred_element_type=F32)
    o_ref[...] = x1 + _rms(acc, gpo_ref[...])


def _back(x, a, att, ga, gb, w_a, w_b, w_o, gpm, gpf, w_g, w_u, w_d, gpo):
    n = x.shape[0]
    tm = min(TILE_BACK, n)
    row = lambda w: pl.BlockSpec((tm, w), lambda i: (i, 0))
    return pl.pallas_call(
        _back_kernel,
        grid=(n // tm,),
        in_specs=[row(D_MODEL), row(SGU_WIDTH), row(ATTN_WIDTH), row(D_MODEL), row(D_MODEL),
                  _resident(w_a.shape), _resident(w_b.shape), _resident(w_o.shape),
                  _resident(gpm.shape), _resident(gpf.shape),
                  _resident(w_g.shape), _resident(w_u.shape), _resident(w_d.shape), _resident(gpo.shape)],
        out_specs=row(D_MODEL),
        out_shape=jax.ShapeDtypeStruct((n, D_MODEL), F32),
        compiler_params=pltpu.CompilerParams(dimension_semantics=("arbitrary",),
                                             vmem_limit_bytes=VMEM_LIMIT_BACK),
        name="mixer_back_ffn",
    )(x, a, att, ga, gb, w_a, w_b, w_o, gpm, gpf, w_g, w_u, w_d, gpo)


def _front_sample_kernel(x_ref, gpre_ref, win_ref, gsgu_ref, coef_ref, bias_ref,
                         a_ref, vn_ref, q_ref, k_ref, v_ref, ga_ref, gb_ref):
    h = _rms(x_ref[...], gpre_ref[...]).astype(BF16)

    def proj(lo, hi):
        return jnp.dot(h, win_ref[:, lo:hi], preferred_element_type=F32)

    c0 = SGU_WIDTH
    c1 = 2 * SGU_WIDTH
    c2 = c1 + ATTN_WIDTH
    c3 = c2 + ATTN_WIDTH
    c4 = c3 + ATTN_WIDTH
    c5 = c4 + D_MODEL
    u = proj(0, c0)
    vn = _rms(proj(c0, c1), gsgu_ref[...])
    vn_ref[...] = vn
    a_ref[...] = (u * (vn * coef_ref[...] + bias_ref[...])).astype(BF16)
    q_ref[...] = proj(c1, c2)
    k_ref[...] = proj(c2, c3)
    v_ref[...] = proj(c3, c4)
    ga_ref[...] = proj(c4, c5)
    gb_ref[...] = proj(c5, c5 + D_MODEL)


def _head_sums(row):
    return [jnp.sum(row[:, h * HEAD_DIM:(h + 1) * HEAD_DIM], axis=1, keepdims=True) for h in range(N_HEADS)]


def _tree_sum(xs):
    while len(xs) > 1:
        xs = [xs[i] + xs[i + 1] for i in range(0, len(xs) - 1, 2)] + ([xs[-1]] if len(xs) % 2 else [])
    return xs[0]


def _paged_key_scores(ck_rows, row_idx, q_lanes, *, n_samples, n_pages):
    chunks = n_pages * SC_CHUNKS_PER_PAGE
    per_worker = n_samples // SC_WORKERS
    idx_len = chunks * SC_SLABS_PER_CHUNK
    q_len = ATTN_WIDTH * SC_LANES
    page_len = N_HEADS * PAGE_SIZE
    flush_len = SC_FLUSH_PAGES * page_len
    sample_len = n_pages * page_len
    rows_per_chunk = SC_SLABS_PER_CHUNK * SC_SLAB_ROWS
    key_chunks = PAGE_SIZE // SC_LANES
    assert SC_SLAB_ROWS % SC_ROW_UNROLL == 0 and HEAD_DIM % SC_ROW_UNROLL == 0 and n_pages % SC_FLUSH_PAGES == 0
    ring = SC_CHUNKS_PER_PAGE
    mesh = plsc.VectorSubcoreMesh(core_axis_name="c", subcore_axis_name="s")

    @functools.partial(
        pl.kernel, mesh=mesh,
        out_type=jax.ShapeDtypeStruct((n_samples * sample_len,), F32),
        scratch_types=[pltpu.VMEM((idx_len,), jnp.int32),
                       pltpu.VMEM((q_len,), F32),
                       pltpu.VMEM((ring, SC_SLABS_PER_CHUNK, SC_SLAB_ROWS, PAGE_SIZE), F32),
                       pltpu.VMEM((flush_len,), F32),
                       pltpu.SemaphoreType.DMA((ring,))],
        name="sample_key_scores_sc",
    )
    def scores(ck_hbm, idx_hbm, q_hbm, out_hbm, idx_v, q_v, rows_v, out_v, sems):
        wid = lax.axis_index("s") * SC_CORES + lax.axis_index("c")

        def fetch(c, slot):
            ids = idx_v.at[pl.ds(pl.multiple_of(c * SC_SLABS_PER_CHUNK, SC_SLABS_PER_CHUNK), SC_SLABS_PER_CHUNK)]
            return pltpu.make_async_copy(ck_hbm.at[ids], rows_v.at[slot], sems.at[slot])

        @pl.loop(0, per_worker)
        def _(t):
            n = wid * per_worker + t
            pltpu.sync_copy(idx_hbm.at[pl.ds(pl.multiple_of(n * idx_len, idx_len), idx_len)], idx_v)
            pltpu.sync_copy(q_hbm.at[pl.ds(pl.multiple_of(n * q_len, q_len), q_len)], q_v)
            for slot in range(ring):
                fetch(slot, slot).start()

            @pl.loop(0, n_pages)
            def _(page):
                c0 = page * ring
                page_in_flush = page % SC_FLUSH_PAGES

                @pl.when(page_in_flush == 0)
                def _():
                    @pl.loop(0, flush_len, step=SC_LANES)
                    def _(o):
                        out_v[pl.ds(pl.multiple_of(o, SC_LANES), SC_LANES)] = jnp.zeros((SC_LANES,), F32)

                for slot in range(ring):
                    fetch(c0 + slot, slot).wait()

                    @pl.loop(0, rows_per_chunk, step=SC_ROW_UNROLL)
                    def _(r0, slot=slot):
                        row0 = slot * rows_per_chunk + r0
                        obase = (page_in_flush * N_HEADS + row0 // HEAD_DIM) * PAGE_SIZE
                        slab = r0 // SC_SLAB_ROWS
                        j0 = r0 % SC_SLAB_ROWS
                        qs = [q_v[pl.ds(pl.multiple_of((row0 + jj) * SC_LANES, SC_LANES), SC_LANES)]
                              for jj in range(SC_ROW_UNROLL)]
                        sums = []
                        for kk in range(key_chunks):
                            terms = [rows_v[slot, slab, j0 + jj, pl.ds(kk * SC_LANES, SC_LANES)] * qs[jj]
                                     for jj in range(SC_ROW_UNROLL)]
                            sums.append(_tree_sum(terms))
                        for kk in range(key_chunks):
                            ooff = pl.multiple_of(obase + kk * SC_LANES, SC_LANES)
                            plsc.addupdate(out_v.at[pl.ds(ooff, SC_LANES)], sums[kk])

                    @pl.when(c0 + slot + ring < chunks)
                    def _():
                        fetch(c0 + slot + ring, slot).start()

                @pl.when(page_in_flush == SC_FLUSH_PAGES - 1)
                def _():
                    off = n * sample_len + (page - (SC_FLUSH_PAGES - 1)) * page_len
                    pltpu.sync_copy(out_v, out_hbm.at[pl.ds(pl.multiple_of(off, flush_len), flush_len)])

    return scores(ck_rows, row_idx, q_lanes)


def _split3_bf16(x):
    hi = x.astype(BF16)
    r1 = x - hi.astype(F32)
    mid = r1.astype(BF16)
    lo = (r1 - mid.astype(F32)).astype(BF16)
    return hi, mid, lo


def _dot_exact_onehot(x, onehot):
    return sum(jnp.dot(t, onehot, preferred_element_type=F32) for t in _split3_bf16(x))


def _topk_kernel(s_ref, q_ref, k_ref, idx_ref, *, n_past_blk):
    per_step = s_ref.shape[0]
    sub = 8
    s = s_ref[...].reshape(per_step * n_past_blk, PAGES_PER_BLOCK, N_HEADS, PAGE_SIZE)
    per_blk = s[:, 0]
    for pg in range(1, PAGES_PER_BLOCK):
        per_blk = per_blk + s[:, pg]
    sums = jnp.sum(per_blk, axis=2, keepdims=True)
    h_i = lax.broadcasted_iota(jnp.int32, (N_HEADS, LANES), 0)
    l_i = lax.broadcasted_iota(jnp.int32, (N_HEADS, LANES), 1)
    diag = jnp.where(h_i == l_i, 1.0, 0.0)
    gates = jnp.sum(sums * diag, axis=1)
    r_w = lax.broadcasted_iota(jnp.int32, (ATTN_WIDTH, LANES), 0)
    c_w = lax.broadcasted_iota(jnp.int32, (ATTN_WIDTH, LANES), 1)
    head_sum = jnp.where(r_w // HEAD_DIM == c_w, 1.0, 0.0).astype(BF16)
    own = _dot_exact_onehot(q_ref[...] * k_ref[...], head_sum)
    blk_i = lax.broadcasted_iota(jnp.int32, (n_past_blk + sub, LANES), 0)
    blk_f = blk_i.astype(F32)
    out_row = lax.broadcasted_iota(jnp.int32, (sub, LANES), 0)
    for i in range(per_step):
        gate = jnp.concatenate([gates[i * n_past_blk:(i + 1) * n_past_blk],
                                jnp.broadcast_to(own[i:i + 1], (sub, LANES))], axis=0) * (1.0 / MOBA_BLOCK)
        gm = jnp.where(blk_i < n_past_blk, gate, -jnp.inf)
        out = jnp.zeros((sub, LANES), F32)
        for r in range(MOBA_TOPK):
            mx = jnp.max(gm, axis=0, keepdims=True)
            pick = jnp.min(jnp.where(gm == mx, blk_f, float(LANES)), axis=0, keepdims=True)
            out = jnp.where(out_row == r, pick, out)
            gm = jnp.where(blk_f == pick, -jnp.inf, gm)
        idx_ref[i] = out.astype(jnp.int32)


def _decode_attn_kernel(pt_ref, idx_ref, s_ref, q_ref, k_ref, v_ref, cv_ref, o_ref,
                        vbuf_ref, vsem_ref, *, n_samples, n_pages, past_len):
    n = pl.program_id(0)
    n_tiles = MOBA_TOPK * PAGES_PER_BLOCK
    per_sample = N_HEADS * n_tiles

    def pages(sample, h, j):
        blk = idx_ref[sample * (N_HEADS * MOBA_TOPK) + h * MOBA_TOPK + j]
        return [blk * PAGES_PER_BLOCK + pg for pg in range(PAGES_PER_BLOCK)]

    def copies(sample, slot):
        out = []
        for h in range(N_HEADS):
            for j in range(MOBA_TOPK):
                for pg, page in enumerate(pages(sample, h, j)):
                    phys = pt_ref[sample * n_pages + page]
                    i = (h * MOBA_TOPK + j) * PAGES_PER_BLOCK + pg
                    out.append(pltpu.make_async_copy(cv_ref.at[phys, h], vbuf_ref.at[slot, i], vsem_ref.at[slot]))
        return out

    @pl.when(n == 0)
    def _():
        for cp in copies(0, 0):
            cp.start()

    @pl.when(n + 1 < n_samples)
    def _():
        for cp in copies(n + 1, (n + 1) % 2):
            cp.start()

    slot = n % 2
    for i in range(per_sample):
        pltpu.make_async_copy(cv_ref.at[0, 0], vbuf_ref.at[slot, i], vsem_ref.at[slot]).wait()

    q_row = q_ref[pl.ds(n, 1), :]
    k_row = k_ref[pl.ds(n, 1), :]
    v_row = v_ref[pl.ds(n, 1), :]
    s_new = _head_sums(q_row * k_row)
    lane_row = lax.broadcasted_iota(jnp.int32, (1, PAGE_SIZE), 1)
    lane_w = lax.broadcasted_iota(jnp.int32, (1, ATTN_WIDTH), 1)
    acc_heads = []
    p_new_row = jnp.zeros((1, ATTN_WIDTH), F32)
    for h, slope in enumerate(_slopes()):
        scores = []
        for j in range(MOBA_TOPK):
            for page in pages(n, h, j):
                s = s_ref[0, page, pl.ds(h, 1), :] * SCALE
                key_pos = page * PAGE_SIZE + lane_row
                scores.append(s - slope * (past_len - key_pos).astype(F32))
        s_n = s_new[h] * SCALE
        m = s_n
        for s in scores:
            m = jnp.maximum(m, jnp.max(s, axis=1, keepdims=True))
        p_n = jnp.exp(s_n - m)
        l = p_n
        acc = jnp.zeros((HEAD_DIM, PAGE_SIZE), F32)
        for t, s in enumerate(scores):
            p = jnp.exp(s - m)
            l = l + jnp.sum(p, axis=1, keepdims=True)
            acc = acc + vbuf_ref[slot, h * n_tiles + t] * p
        acc_heads.append(acc / l)
        p_new_row = jnp.where(lane_w // HEAD_DIM == h, p_n / l, p_new_row)
    acc_all = jnp.concatenate(acc_heads, axis=0)
    o_row = jnp.sum(acc_all.T, axis=0, keepdims=True) + p_new_row * v_row
    o_ref[0] = o_row


def kernel(x_prompt, x_sample, cache_k, cache_v, page_table, g_pre_mix, w_in, g_sgu, w_spatial, b_spatial,
           w_a, w_b, w_o, g_post_mix, g_pre_ffn, w_gate, w_up, w_down, g_post_ffn):
    depth = w_in.shape[0]
    assert depth == 1, "kernels are written for a single layer"
    bsz, seq, _ = x_prompt.shape
    n_samples, dec_seq, _ = x_sample.shape
    assert dec_seq == 1
    n_pool = cache_k.shape[1]
    n_pages = page_table.shape[1]
    past_len = n_pages * PAGE_SIZE
    n_past_blk = past_len // MOBA_BLOCK
    n_blk = seq // MOBA_BLOCK
    assert n_blk * N_HEADS <= HEAD_DIM and n_past_blk < LANES
    assert n_samples % SC_WORKERS == 0 and n_samples % TOPK_PER_STEP == 0 and n_pages % PAGES_PER_BLOCK == 0

    l = 0
    w_in_b = w_in[l].astype(BF16)
    back_w = (w_a[l].astype(BF16), w_b[l].astype(BF16), w_o[l].astype(BF16), g_post_mix[l][None], g_pre_ffn[l][None],
              w_gate[l].astype(BF16), w_up[l].astype(BF16), w_down[l].astype(BF16), g_post_ffn[l][None])
    gpre = g_pre_mix[l][None]
    gsgu = g_sgu[l][None]
    in_cols = w_in_b.shape[1]

    xs = x_sample.reshape(n_samples, D_MODEL)
    coef = jnp.repeat(w_spatial[l, :, 0, 0], CHUNK)[None]
    bias = jnp.repeat(b_spatial[l, :, 0], CHUNK)[None]
    full = lambda shape: pl.BlockSpec(shape, lambda *_: (0,) * len(shape))
    s_w = lambda w, dt: jax.ShapeDtypeStruct((n_samples, w), dt)
    a_s, vn_s, q_s, k_s, v_s, ga_s, gb_s = pl.pallas_call(
        _front_sample_kernel,
        grid=(1,),
        in_specs=[full((n_samples, D_MODEL)), full((1, D_MODEL)), full((D_MODEL, in_cols)), full((1, SGU_WIDTH)),
                  full((1, SGU_WIDTH)), full((1, SGU_WIDTH))],
        out_specs=[full((n_samples, SGU_WIDTH)), full((n_samples, SGU_WIDTH)), full((n_samples, ATTN_WIDTH)),
                   full((n_samples, ATTN_WIDTH)), full((n_samples, ATTN_WIDTH)),
                   full((n_samples, D_MODEL)), full((n_samples, D_MODEL))],
        out_shape=[s_w(SGU_WIDTH, BF16), s_w(SGU_WIDTH, F32), s_w(ATTN_WIDTH, F32), s_w(ATTN_WIDTH, F32),
                   s_w(ATTN_WIDTH, F32), s_w(D_MODEL, F32), s_w(D_MODEL, F32)],
        compiler_params=pltpu.CompilerParams(dimension_semantics=("arbitrary",),
                                             vmem_limit_bytes=VMEM_LIMIT_FRONT),
        name="sample_front",
    )(xs, gpre, w_in_b, gsgu, coef, bias)

    ck_t = jnp.transpose(cache_k[l], (0, 2, 3, 1))
    cv_t = jnp.transpose(cache_v[l], (0, 2, 3, 1))
    pt_flat = page_table.reshape(-1)
    slabs_per_page = ATTN_WIDTH // SC_SLAB_ROWS
    ck_rows = ck_t.reshape(n_pool * slabs_per_page, SC_SLAB_ROWS, PAGE_SIZE)
    row_idx = (page_table[:, :, None] * slabs_per_page + jnp.arange(slabs_per_page, dtype=jnp.int32)).reshape(-1)
    q_lanes = jnp.repeat(q_s, SC_LANES, axis=1).reshape(-1)
    key_scores = _paged_key_scores(ck_rows, row_idx, q_lanes, n_samples=n_samples, n_pages=n_pages)
    key_scores = key_scores.reshape(n_samples, n_pages, N_HEADS, PAGE_SIZE)

    n_tiles = seq // TILE_FRONT
    tile3 = lambda w: pl.BlockSpec((1, TILE_FRONT, w), lambda b, j: (b, j, 0))
    tile3t = lambda w: pl.BlockSpec((1, w, TILE_FRONT), lambda b, j: (b, 0, j))
    a_p, qext, kt, vt, vb, ga, gb = pl.pallas_call(
        _front_kernel,
        grid=(bsz, n_tiles),
        in_specs=[tile3(D_MODEL), _resident((1, D_MODEL)), _resident((D_MODEL, in_cols)), _resident((1, SGU_WIDTH)),
                  _resident((SGU_GROUPS, CHUNK, CHUNK)), _resident((CHUNK, SGU_GROUPS))],
        out_specs=[tile3(SGU_WIDTH),
                   pl.BlockSpec((1, N_HEADS, TILE_FRONT, PAIR), lambda b, j: (b, 0, j, 0)),
                   tile3t(ATTN_WIDTH), tile3t(ATTN_WIDTH), tile3(ATTN_WIDTH), tile3(D_MODEL), tile3(D_MODEL)],
        out_shape=[jax.ShapeDtypeStruct((bsz, seq, SGU_WIDTH), BF16),
                   jax.ShapeDtypeStruct((bsz, N_HEADS, seq, PAIR), BF16),
                   jax.ShapeDtypeStruct((bsz, ATTN_WIDTH, seq), F32),
                   jax.ShapeDtypeStruct((bsz, ATTN_WIDTH, seq), F32),
                   jax.ShapeDtypeStruct((bsz, seq, ATTN_WIDTH), BF16),
                   jax.ShapeDtypeStruct((bsz, seq, D_MODEL), F32),
                   jax.ShapeDtypeStruct((bsz, seq, D_MODEL), F32)],
        scratch_shapes=[pltpu.VMEM((n_blk, ATTN_WIDTH), F32)],
        compiler_params=pltpu.CompilerParams(dimension_semantics=("arbitrary", "arbitrary"),
                                             vmem_limit_bytes=VMEM_LIMIT_FRONT),
        name="prompt_front",
    )(x_prompt, gpre, w_in_b, gsgu, w_spatial[l], b_spatial[l].T)

    kext = _key_ext_const(seq)
    att_p = pl.pallas_call(
        _attn_kernel,
        grid=(bsz, N_PAIRS),
        in_specs=[pl.BlockSpec((1, 2, seq, PAIR), lambda b, p: (b, p, 0, 0)),
                  pl.BlockSpec((1, PAIR, seq), lambda b, p: (b, p, 0)),
                  pl.BlockSpec((1, seq, PAIR), lambda b, p: (b, 0, p)),
                  pl.BlockSpec((2, PAIR, seq), lambda b, p: (p, 0, 0))],
        out_specs=pl.BlockSpec((1, seq, PAIR), lambda b, p: (b, 0, p)),
        out_shape=jax.ShapeDtypeStruct((bsz, seq, ATTN_WIDTH), BF16),
        scratch_shapes=[pltpu.VMEM((2, PAIR, seq), BF16)],
        compiler_params=pltpu.CompilerParams(dimension_semantics=("arbitrary", "arbitrary"),
                                             vmem_limit_bytes=VMEM_LIMIT_ATTN),
        name="prompt_moba",
    )(qext, kt, vb, kext)

    n_tok = bsz * seq
    flat = lambda z: z.reshape(n_tok, z.shape[-1])
    y_prompt = _back(flat(x_prompt), flat(a_p), flat(att_p), flat(ga), flat(gb), *back_w).reshape(bsz, seq, D_MODEL)

    idx = pl.pallas_call(
        functools.partial(_topk_kernel, n_past_blk=n_past_blk),
        grid=(n_samples // TOPK_PER_STEP,),
        in_specs=[pl.BlockSpec((TOPK_PER_STEP, n_pages, N_HEADS, PAGE_SIZE), lambda i: (i, 0, 0, 0)),
                  pl.BlockSpec((TOPK_PER_STEP, ATTN_WIDTH), lambda i: (i, 0)),
                  pl.BlockSpec((TOPK_PER_STEP, ATTN_WIDTH), lambda i: (i, 0))],
        out_specs=pl.BlockSpec((TOPK_PER_STEP, 8, LANES), lambda i: (i, 0, 0)),
        out_shape=jax.ShapeDtypeStruct((n_samples, 8, LANES), jnp.int32),
        compiler_params=pltpu.CompilerParams(dimension_semantics=("arbitrary",)),
        name="sample_block_topk",
    )(key_scores, q_s, k_s)
    idx_flat = jnp.transpose(idx[:, :MOBA_TOPK, :N_HEADS], (0, 2, 1)).reshape(-1)
    n_gather = N_HEADS * MOBA_TOPK * PAGES_PER_BLOCK
    att_s = pl.pallas_call(
        functools.partial(_decode_attn_kernel, n_samples=n_samples, n_pages=n_pages, past_len=past_len),
        grid_spec=pltpu.PrefetchScalarGridSpec(
            num_scalar_prefetch=2,
            grid=(n_samples,),
            in_specs=[pl.BlockSpec((1, n_pages, N_HEADS, PAGE_SIZE), lambda i, pt, ix: (i, 0, 0, 0)),
                      pl.BlockSpec((n_samples, ATTN_WIDTH), lambda i, pt, ix: (0, 0)),
                      pl.BlockSpec((n_samples, ATTN_WIDTH), lambda i, pt, ix: (0, 0)),
                      pl.BlockSpec((n_samples, ATTN_WIDTH), lambda i, pt, ix: (0, 0)),
                      pl.BlockSpec(memory_space=pl.ANY)],
            out_specs=pl.BlockSpec((1, 1, ATTN_WIDTH), lambda i, pt, ix: (i, 0, 0)),
            scratch_shapes=[pltpu.VMEM((2, n_gather, HEAD_DIM, PAGE_SIZE), F32),
                            pltpu.SemaphoreType.DMA((2,))]),
        out_shape=jax.ShapeDtypeStruct((n_samples, 1, ATTN_WIDTH), F32),
        compiler_params=pltpu.CompilerParams(dimension_semantics=("arbitrary",),
                                             vmem_limit_bytes=VMEM_LIMIT_DECODE),
        name="sample_moba",
    )(pt_flat, idx_flat, key_scores, q_s, k_s, v_s, cv_t)
    att_s = att_s.reshape(n_samples, ATTN_WIDTH).astype(BF16)

    y_sample = _back(xs, a_s, att_s, ga_s, gb_s, *back_w).reshape(n_samples, 1, D_MODEL)

    heads_t = lambda zt: jnp.transpose(zt.reshape(1, bsz, N_HEADS, HEAD_DIM, seq), (0, 1, 4, 2, 3))
    heads_s = lambda z: z.reshape(1, n_samples, 1, N_HEADS, HEAD_DIM)
    return (y_prompt, y_sample, heads_t(kt), heads_t(vt), heads_s(k_s), heads_s(v_s),
            vn_s.reshape(1, n_samples, 1, SGU_WIDTH))
```
